```python
import numpy as np
import jax
import jax.numpy as jnp
from jax import lax

D_MODEL = 2048
BATCH = 8
SEQ = 4096
DEPTH = 2

RMS_EPS = 1e-6
MIX_WIDTH = D_MODEL

S5_WIDTH = D_MODEL // 4
S5_GROUP_CH = 16
S5_GROUPS = S5_WIDTH // S5_GROUP_CH
S5_STATE = 64

SSD_WIDTH = MIX_WIDTH - S5_WIDTH
SSD_HEADDIM = 64
SSD_HEADS = SSD_WIDTH // SSD_HEADDIM
SSD_GROUPS = 4
SSD_HPG = SSD_HEADS // SSD_GROUPS
SSD_STATE = 128
SSD_CONV = 4
SSD_CHUNK = 128
SSD_CONV_CH = SSD_WIDTH + 2 * SSD_GROUPS * SSD_STATE

SWA_HEADDIM = 64
SWA_WIDTH = MIX_WIDTH // 2
SWA_HEADS = SWA_WIDTH // SWA_HEADDIM
SWA_KV_HEADS = 4
SWA_HPG = SWA_HEADS // SWA_KV_HEADS
SWA_WINDOW = 128

GLA_VALUE_WIDTH = MIX_WIDTH - SWA_WIDTH
GLA_KEY_WIDTH = GLA_VALUE_WIDTH // 2
GLA_HEADS = 4
GLA_HEAD_K = GLA_KEY_WIDTH // GLA_HEADS
GLA_HEAD_V = GLA_VALUE_WIDTH // GLA_HEADS
GLA_GATE_RANK = 16
GLA_GATE_NORMALIZER = 16.0
GLA_CHUNK = 64

FFN_HIDDEN = -(-(8 * D_MODEL) // (3 * 256)) * 256

IN0_SIZES = (S5_WIDTH, SSD_WIDTH, SSD_CONV_CH, SSD_HEADS)
IN1_SIZES = (SWA_HEADS * SWA_HEADDIM, SWA_KV_HEADS * SWA_HEADDIM, SWA_KV_HEADS * SWA_HEADDIM,
             GLA_KEY_WIDTH, GLA_KEY_WIDTH, GLA_VALUE_WIDTH, GLA_VALUE_WIDTH, GLA_GATE_RANK)

kernel_name = "hybrid_s5_ssd_swa_gla_block"


def split_cols(t, sizes):
    return jnp.split(t, np.cumsum(sizes)[:-1].tolist(), axis=-1)


def rmsnorm(t, w):
    tf = t.astype(jnp.float32)
    y = tf * lax.rsqrt(jnp.mean(tf * tf, axis=-1, keepdims=True) + RMS_EPS)
    return (y * w.astype(jnp.float32)).astype(t.dtype)


def swiglu(t, w1, w3, w2):
    return (jax.nn.silu(t @ w1) * (t @ w3)) @ w2


def causal_depthwise_conv(t, w, bias):
    k_w, ch = w.shape
    out = lax.conv_general_dilated(t, w[:, None, :].astype(t.dtype), window_strides=(1,),
                                   padding=[(k_w - 1, 0)],
                                   dimension_numbers=('NWC', 'WIO', 'NWC'),
                                   feature_group_count=ch)
    return out + bias


def s5_mixer(u, A_re, A_im, log_dt, B_re, B_im, C_re, C_im, D, glu_w, glu_b):
    f32 = jnp.float32
    b, L, _ = u.shape
    uf = u.astype(f32)
    ug = uf.reshape(b, L, S5_GROUPS, S5_GROUP_CH)
    a_re, a_im = A_re.astype(f32), A_im.astype(f32)
    dt = jnp.exp(log_dt.astype(f32))[:, None]
    mag = jnp.exp(a_re * dt)
    ab_re = mag * jnp.cos(a_im * dt)
    ab_im = mag * jnp.sin(a_im * dt)
    e_re, e_im = ab_re - 1.0, ab_im
    den = a_re * a_re + a_im * a_im
    coef_re = (e_re * a_re + e_im * a_im) / den
    coef_im = (e_im * a_re - e_re * a_im) / den
    bu_re = jnp.einsum('blgc,gnc->blgn', ug, B_re.astype(f32))
    bu_im = jnp.einsum('blgc,gnc->blgn', ug, B_im.astype(f32))
    x_re = coef_re * bu_re - coef_im * bu_im
    x_im = coef_re * bu_im + coef_im * bu_re
    a_re_seq = jnp.broadcast_to(ab_re, (1, L) + ab_re.shape)
    a_im_seq = jnp.broadcast_to(ab_im, (1, L) + ab_im.shape)

    def combine(e1, e2):
        a1r, a1i, b1r, b1i = e1
        a2r, a2i, b2r, b2i = e2
        return (a2r * a1r - a2i * a1i,
                a2r * a1i + a2i * a1r,
                a2r * b1r - a2i * b1i + b2r,
                a2r * b1i + a2i * b1r + b2i)

    _, _, h_re, h_im = lax.associative_scan(combine, (a_re_seq, a_im_seq, x_re, x_im), axis=1)
    y = (jnp.einsum('blgn,gcn->blgc', h_re, C_re.astype(f32))
         - jnp.einsum('blgn,gcn->blgc', h_im, C_im.astype(f32)))
    y = y.reshape(b, L, S5_WIDTH) + D.astype(f32) * uf
    y = jax.nn.gelu(y)
    y = y * jax.nn.sigmoid(y @ glu_w.astype(f32) + glu_b.astype(f32))
    return y.astype(u.dtype)


def ssd_mixer(z, xbc, dt_raw, conv_w, conv_b, dt_bias, A_log, D, norm_w):
    f32 = jnp.float32
    b, L, _ = z.shape
    nc, Q = L // SSD_CHUNK, SSD_CHUNK
    G, R, P, N = SSD_GROUPS, SSD_HPG, SSD_HEADDIM, SSD_STATE
    xbc = jax.nn.silu(causal_depthwise_conv(xbc, conv_w, conv_b)).astype(f32)
    xs, Bm, Cm = split_cols(xbc, (SSD_WIDTH, G * N, G * N))
    xs = xs.reshape(b, nc, Q, G, R, P)
    Bm = Bm.reshape(b, nc, Q, G, N)
    Cm = Cm.reshape(b, nc, Q, G, N)
    dt = jax.nn.softplus(dt_raw.astype(f32) + dt_bias.astype(f32)).reshape(b, nc, Q, G, R)
    A = -jnp.exp(A_log.astype(f32)).reshape(G, R)
    cs = jnp.moveaxis(jnp.cumsum(dt * A, axis=2), 2, -1)
    xdt = xs * dt[..., None]
    pos = jnp.arange(Q)
    causal = pos[:, None] >= pos[None, :]
    seg = jnp.where(causal, cs[..., :, None] - cs[..., None, :], -jnp.inf)
    cb = jnp.einsum('bclgn,bcsgn->bcgls', Cm, Bm)
    y_diag = jnp.einsum('bcgls,bcgrls,bcsgrp->bclgrp', cb, jnp.exp(seg), xdt)
    cs_last = cs[..., -1:]
    states = jnp.einsum('bclgn,bcgrl,bclgrp->bcgrpn', Bm, jnp.exp(cs_last - cs), xdt)
    chunk_decay = jnp.exp(cs_last[..., 0])

    def step(S, inp):
        dec, st = inp
        return dec[..., None, None] * S + st, S

    S0 = jnp.zeros((b, G, R, P, N), f32)
    _, S_in = lax.scan(step, S0, (jnp.moveaxis(chunk_decay, 1, 0), jnp.moveaxis(states, 1, 0)))
    S_in = jnp.moveaxis(S_in, 0, 1)
    y_off = jnp.einsum('bclgn,bcgrpn,bcgrl->bclgrp', Cm, S_in, jnp.exp(cs))
    y = y_diag + y_off + D.astype(f32).reshape(G, R)[..., None] * xs
    y = y.reshape(b, L, SSD_WIDTH) * jax.nn.silu(z.astype(f32))
    y = rmsnorm(y.reshape(b, L, G, SSD_WIDTH // G), norm_w.reshape(G, SSD_WIDTH // G))
    return y.reshape(b, L, SSD_WIDTH).astype(z.dtype)


def swa_sink_attention(q, k, v, q_norm_w, k_norm_w, sinks):
    f32 = jnp.float32
    b, L, _ = q.shape
    blk = SWA_WINDOW
    nb = L // blk
    H, KV, R, HD = SWA_HEADS, SWA_KV_HEADS, SWA_HPG, SWA_HEADDIM
    q = rmsnorm(q.reshape(b, L, H, HD), q_norm_w).reshape(b, nb, blk, KV, R, HD)
    k = rmsnorm(k.reshape(b, L, KV, HD), k_norm_w).reshape(b, nb, blk, KV, HD)
    v = v.reshape(b, nb, blk, KV, HD)

    def with_prev(t):
        prev = jnp.concatenate([jnp.zeros_like(t[:, :1]), t[:, :-1]], axis=1)
        return jnp.concatenate([prev, t], axis=2)

    kw, vw = with_prev(k), with_prev(v)
    s = jnp.einsum('bnqhrd,bnkhd->bnhrqk', q, kw).astype(f32) * (HD ** -0.5)
    qpos = jnp.arange(blk)[:, None]
    kpos = jnp.arange(2 * blk)[None, :] - blk
    rel = qpos - kpos
    blk_start = (jnp.arange(nb) * blk)[:, None, None]
    valid = (rel >= 0) & (rel < SWA_WINDOW) & (blk_start + kpos >= 0)
    s = jnp.where(valid[None, :, None, None], s, -jnp.inf)
    sink = sinks.astype(f32).reshape(KV, R)[None, None, :, :, None, None]
    m = jnp.maximum(jnp.max(s, axis=-1, keepdims=True), sink)
    p = jnp.exp(s - m)
    denom = jnp.sum(p, axis=-1, keepdims=True) + jnp.exp(sink - m)
    o = jnp.einsum('bnhrqk,bnkhd->bnqhrd', (p / denom).astype(vw.dtype), vw)
    return o.reshape(b, L, SWA_WIDTH)


def gla_mixer(q, k, v, gout, g_lr, gate_w2, gate_b, norm_w):
    f32 = jnp.float32
    b, L, _ = q.shape
    C = GLA_CHUNK
    nc = L // C
    H, DK, DV = GLA_HEADS, GLA_HEAD_K, GLA_HEAD_V
    q = q.astype(f32).reshape(b, nc, C, H, DK) * (DK ** -0.5)
    k = k.astype(f32).reshape(b, nc, C, H, DK)
    v = v.astype(f32).reshape(b, nc, C, H, DV)
    gk = jax.nn.log_sigmoid((g_lr @ gate_w2 + gate_b).astype(f32)) / GLA_GATE_NORMALIZER
    bc = jnp.cumsum(gk.reshape(b, nc, C, H, DK), axis=2)
    q_t = q * jnp.exp(bc)
    k_t = k * jnp.exp(-bc)
    pos = jnp.arange(C)
    causal = pos[:, None] >= pos[None, :]
    att = jnp.where(causal, jnp.einsum('bclhd,bcshd->bchls', q_t, k_t), 0.0)
    o_intra = jnp.einsum('bchls,bcshv->bclhv', att, v)
    b_last = bc[:, :, -1:]
    chunk_kv = jnp.einsum('bclhd,bclhv->bchdv', k * jnp.exp(b_last - bc), v)
    chunk_decay = jnp.exp(b_last[:, :, 0])

    def step(S, inp):
        dec, kv = inp
        return dec[..., None] * S + kv, S

    S0 = jnp.zeros((b, H, DK, DV), f32)
    _, S_in = lax.scan(step, S0, (jnp.moveaxis(chunk_decay, 1, 0), jnp.moveaxis(chunk_kv, 1, 0)))
    S_in = jnp.moveaxis(S_in, 0, 1)
    o = o_intra + jnp.einsum('bclhd,bchdv->bclhv', q_t, S_in)
    o = rmsnorm(o, norm_w).reshape(b, L, GLA_VALUE_WIDTH)
    return (o * jax.nn.silu(gout.astype(f32))).astype(gout.dtype)


def setup_inputs(seed: int = 0) -> dict:
    key = jax.random.key(seed)
    ks = iter(jax.random.split(key, 48))
    f32 = jnp.float32

    def nrm(shape, scale):
        return jax.random.normal(next(ks), shape, f32) * scale

    def gain(shape):
        return 1.0 + nrm(shape, 0.01)

    def log_uniform(shape, lo, hi):
        u = jax.random.uniform(next(ks), shape, f32)
        return u * (np.log(hi) - np.log(lo)) + np.log(lo)

    x = nrm((BATCH, SEQ, D_MODEL), 1.0)
    norm0_mix = gain((D_MODEL,))
    w_in0 = nrm((D_MODEL, sum(IN0_SIZES)), D_MODEL ** -0.5)
    s5_A_re = -0.5 + nrm((S5_GROUPS, S5_STATE), 0.01)
    s5_A_im = jnp.pi * jnp.arange(S5_STATE, dtype=f32)[None, :] + nrm((S5_GROUPS, S5_STATE), 0.01)
    s5_log_dt = log_uniform((S5_GROUPS,), 1e-3, 1e-1)
    s5_B_re = nrm((S5_GROUPS, S5_STATE, S5_GROUP_CH), (2 * S5_GROUP_CH) ** -0.5)
    s5_B_im = nrm((S5_GROUPS, S5_STATE, S5_GROUP_CH), (2 * S5_GROUP_CH) ** -0.5)
    s5_C_re = nrm((S5_GROUPS, S5_GROUP_CH, S5_STATE), S5_STATE ** -0.5)
    s5_C_im = nrm((S5_GROUPS, S5_GROUP_CH, S5_STATE), S5_STATE ** -0.5)
    s5_D = nrm((S5_WIDTH,), 1.0)
    s5_glu_w = nrm((S5_WIDTH, S5_WIDTH), S5_WIDTH ** -0.5)
    s5_glu_b = nrm((S5_WIDTH,), 0.01)
    ssd_conv_w = nrm((SSD_CONV, SSD_CONV_CH), SSD_CONV ** -0.5)
    ssd_conv_b = nrm((SSD_CONV_CH,), 0.01)
    dt0 = jnp.exp(log_uniform((SSD_HEADS,), 1e-3, 1e-1))
    ssd_dt_bias = dt0 + jnp.log(-jnp.expm1(-dt0))
    ssd_A_log = jnp.log(jax.random.uniform(next(ks), (SSD_HEADS,), f32, 1.0, 16.0))
    ssd_D = gain((SSD_HEADS,))
    ssd_norm_w = gain((SSD_WIDTH,))
    w_out0 = nrm((MIX_WIDTH, D_MODEL), MIX_WIDTH ** -0.5)
    norm0_ffn = gain((D_MODEL,))
    ffn0_w1 = nrm((D_MODEL, FFN_HIDDEN), D_MODEL ** -0.5)
    ffn0_w3 = nrm((D_MODEL, FFN_HIDDEN), D_MODEL ** -0.5)
    ffn0_w2 = nrm((FFN_HIDDEN, D_MODEL), FFN_HIDDEN ** -0.5)
    norm1_mix = gain((D_MODEL,))
    w_in1 = nrm((D_MODEL, sum(IN1_SIZES)), D_MODEL ** -0.5)
    swa_q_norm = gain((SWA_HEADDIM,))
    swa_k_norm = gain((SWA_HEADDIM,))
    swa_sinks = nrm((SWA_HEADS,), 0.5)
    gla_gate_w2 = nrm((GLA_GATE_RANK, GLA_KEY_WIDTH), GLA_GATE_RANK ** -0.5)
    gla_gate_b = nrm((GLA_KEY_WIDTH,), 0.01)
    gla_norm_w = gain((GLA_HEAD_V,))
    w_out1 = nrm((MIX_WIDTH, D_MODEL), MIX_WIDTH ** -0.5)
    norm1_ffn = gain((D_MODEL,))
    ffn1_w1 = nrm((D_MODEL, FFN_HIDDEN), D_MODEL ** -0.5)
    ffn1_w3 = nrm((D_MODEL, FFN_HIDDEN), D_MODEL ** -0.5)
    ffn1_w2 = nrm((FFN_HIDDEN, D_MODEL), FFN_HIDDEN ** -0.5)
    return {
        'x': x,
        'norm0_mix': norm0_mix, 'w_in0': w_in0,
        's5_A_re': s5_A_re, 's5_A_im': s5_A_im, 's5_log_dt': s5_log_dt,
        's5_B_re': s5_B_re, 's5_B_im': s5_B_im, 's5_C_re': s5_C_re, 's5_C_im': s5_C_im,
        's5_D': s5_D, 's5_glu_w': s5_glu_w, 's5_glu_b': s5_glu_b,
        'ssd_conv_w': ssd_conv_w, 'ssd_conv_b': ssd_conv_b, 'ssd_dt_bias': ssd_dt_bias,
        'ssd_A_log': ssd_A_log, 'ssd_D': ssd_D, 'ssd_norm_w': ssd_norm_w,
        'w_out0': w_out0, 'norm0_ffn': norm0_ffn,
        'ffn0_w1': ffn0_w1, 'ffn0_w3': ffn0_w3, 'ffn0_w2': ffn0_w2,
        'norm1_mix': norm1_mix, 'w_in1': w_in1,
        'swa_q_norm': swa_q_norm, 'swa_k_norm': swa_k_norm, 'swa_sinks': swa_sinks,
        'gla_gate_w2': gla_gate_w2, 'gla_gate_b': gla_gate_b, 'gla_norm_w': gla_norm_w,
        'w_out1': w_out1, 'norm1_ffn': norm1_ffn,
        'ffn1_w1': ffn1_w1, 'ffn1_w3': ffn1_w3, 'ffn1_w2': ffn1_w2,
    }


def reference(x, norm0_mix, w_in0, s5_A_re, s5_A_im, s5_log_dt, s5_B_re, s5_B_im, s5_C_re,
              s5_C_im, s5_D, s5_glu_w, s5_glu_b, ssd_conv_w, ssd_conv_b, ssd_dt_bias, ssd_A_log,
              ssd_D, ssd_norm_w, w_out0, norm0_ffn, ffn0_w1, ffn0_w3, ffn0_w2, norm1_mix, w_in1,
              swa_q_norm, swa_k_norm, swa_sinks, gla_gate_w2, gla_gate_b, gla_norm_w, w_out1,
              norm1_ffn, ffn1_w1, ffn1_w3, ffn1_w2):
    h = x
    for layer in range(DEPTH):
        if layer % 2 == 0:
            proj = rmsnorm(h, norm0_mix) @ w_in0
            u, z, xbc, dt_raw = split_cols(proj, IN0_SIZES)
            ya = s5_mixer(u, s5_A_re, s5_A_im, s5_log_dt, s5_B_re, s5_B_im, s5_C_re, s5_C_im,
                          s5_D, s5_glu_w, s5_glu_b)
            yb = ssd_mixer(z, xbc, dt_raw, ssd_conv_w, ssd_conv_b, ssd_dt_bias, ssd_A_log,
                           ssd_D, ssd_norm_w)
            h = h + (jnp.concatenate([ya, yb], axis=-1) @ w_out0).astype(h.dtype)
            h = h + swiglu(rmsnorm(h, norm0_ffn), ffn0_w1, ffn0_w3, ffn0_w2).astype(h.dtype)
        else:
            proj = rmsnorm(h, norm1_mix) @ w_in1
            qc, kc, vc, qd, kd, vd, gout, g_lr = split_cols(proj, IN1_SIZES)
            yc = swa_sink_attention(qc, kc, vc, swa_q_norm, swa_k_norm, swa_sinks)
            yd = gla_mixer(qd, kd, vd, gout, g_lr, gla_gate_w2, gla_gate_b, gla_norm_w)
            h = h + (jnp.concatenate([yc.astype(h.dtype), yd.astype(h.dtype)], axis=-1) @ w_out1).astype(h.dtype)
            h = h + swiglu(rmsnorm(h, norm1_ffn), ffn1_w1, ffn1_w3, ffn1_w2).astype(h.dtype)
    return h
```

```python
import functools

import numpy as np
import jax
import jax.numpy as jnp
from jax import lax
from jax.experimental import pallas as pl
from jax.experimental.pallas import tpu as pltpu

F32 = jnp.float32
BF16 = jnp.bfloat16

LANES = 128
SUBLANES = 8
VMEM_LIMIT_BYTES = 56 * 1024 * 1024

D_MODEL = 2048
RMS_EPS = 1e-6
MIX_WIDTH = D_MODEL

S5_WIDTH = D_MODEL // 4
S5_GROUP_CH = 16
S5_GROUPS = S5_WIDTH // S5_GROUP_CH
S5_STATE = 64
S5_HALF_CH = S5_WIDTH // 2
S5_HALF_GROUPS = S5_GROUPS // 2
S5_HALF_STATES = S5_HALF_GROUPS * S5_STATE
S5_TIME_BLOCK = 128

SSD_WIDTH = MIX_WIDTH - S5_WIDTH
SSD_HEADDIM = 64
SSD_HEADS = SSD_WIDTH // SSD_HEADDIM
SSD_GROUPS = 4
SSD_HPG = SSD_HEADS // SSD_GROUPS
SSD_STATE = 128
SSD_CONV = 4
SSD_CHUNK = 128
SSD_BC_WIDTH = SSD_GROUPS * SSD_STATE
SSD_CONV_CH = SSD_WIDTH + 2 * SSD_BC_WIDTH
SSD_GROUP_WIDTH = SSD_WIDTH // SSD_GROUPS

SWA_HEADDIM = 64
SWA_WIDTH = MIX_WIDTH // 2
SWA_HEADS = SWA_WIDTH // SWA_HEADDIM
SWA_KV_HEADS = 4
SWA_HPG = SWA_HEADS // SWA_KV_HEADS
SWA_WINDOW = 128
SWA_KV_WIDTH = SWA_KV_HEADS * SWA_HEADDIM

GLA_VALUE_WIDTH = MIX_WIDTH - SWA_WIDTH
GLA_KEY_WIDTH = GLA_VALUE_WIDTH // 2
GLA_HEADS = 4
GLA_HEAD_K = GLA_KEY_WIDTH // GLA_HEADS
GLA_HEAD_V = GLA_VALUE_WIDTH // GLA_HEADS
GLA_GATE_RANK = 16
GLA_GATE_NORMALIZER = 16.0
GLA_CHUNK = 64
GLA_TIME_BLOCK = 128

FFN_HIDDEN = -(-(8 * D_MODEL) // (3 * 256)) * 256

PROJ_ROW_BLOCK = 256
FFN_ROW_BLOCK = 512
FFN_HIDDEN_BLOCK = 512
OUT_ROW_BLOCK = 512
PROJ_COL_CHUNK = 512

NEG_BIG = -1e30


def _params(*semantics):
    return pltpu.CompilerParams(dimension_semantics=semantics, vmem_limit_bytes=VMEM_LIMIT_BYTES)


def _vmem_spec():
    return pl.BlockSpec(memory_space=pltpu.VMEM)


def _dot(a, b):
    return jnp.dot(a, b, preferred_element_type=F32)


def _dot_nt(a, b):
    return lax.dot_general(a, b, (((1,), (1,)), ((), ())), preferred_element_type=F32)


def _dot_tn(a, b):
    return lax.dot_general(a, b, (((0,), (0,)), ((), ())), preferred_element_type=F32)


def _split3(v):
    hi = v.astype(BF16)
    r1 = v - hi.astype(F32)
    mid = r1.astype(BF16)
    lo = (r1 - mid.astype(F32)).astype(BF16)
    return hi, mid, lo


def _dot_exact_rhs01(v, m01):
    hi, mid, lo = _split3(v)
    return _dot(hi, m01) + _dot(mid, m01) + _dot(lo, m01)


def _dot_exact_lhs01(m01, v):
    hi, mid, lo = _split3(v)
    return _dot(m01, hi) + _dot(m01, mid) + _dot(m01, lo)


def _sigmoid(x):
    return 1.0 / (1.0 + jnp.exp(-x))


def _silu(x):
    return x * _sigmoid(x)


def _softplus(x):
    return jnp.maximum(x, 0.0) + jnp.log1p(jnp.exp(-jnp.abs(x)))


def _gelu_tanh(x):
    c = np.sqrt(2.0 / np.pi).astype(np.float32)
    return 0.5 * x * (1.0 + jnp.tanh(c * (x + 0.044715 * (x * x * x))))


def _rmsnorm_rows(x, w):
    ms = jnp.mean(x * x, axis=-1, keepdims=True)
    return x * lax.rsqrt(ms + RMS_EPS) * w


def _norm_proj_kernel(pieces, x_ref, g_ref, w_ref, *rest):
    out_refs = rest[:len(pieces)]
    xn_ref = rest[len(pieces)]
    xn_ref[...] = _rmsnorm_rows(x_ref[...], g_ref[...]).astype(BF16)
    for (start, width), o_ref in zip(pieces, out_refs):
        for c0 in range(0, width, PROJ_COL_CHUNK):
            cw = min(PROJ_COL_CHUNK, width - c0)
            o_ref[:, c0:c0 + cw] = _dot(xn_ref[...], w_ref[:, start + c0:start + c0 + cw])


def _norm_proj(x2d, g, w_bf16, pieces, time_major_piece, batch, seq):
    m, d = x2d.shape
    tm = PROJ_ROW_BLOCK
    n_l = seq // tm
    out_shapes, out_specs = [], []
    for idx, (_, width) in enumerate(pieces):
        if idx == time_major_piece:
            out_shapes.append(jax.ShapeDtypeStruct((seq, batch * width), F32))
            out_specs.append(pl.BlockSpec((tm, width), lambda i: (i % n_l, i // n_l)))
        else:
            out_shapes.append(jax.ShapeDtypeStruct((m, width), F32))
            out_specs.append(pl.BlockSpec((tm, width), lambda i: (i, 0)))
    return pl.pallas_call(
        functools.partial(_norm_proj_kernel, tuple(pieces)),
        out_shape=tuple(out_shapes),
        grid=(m // tm,),
        in_specs=[pl.BlockSpec((tm, d), lambda i: (i, 0)),
                  pl.BlockSpec((1, d), lambda i: (0, 0)),
                  _vmem_spec()],
        out_specs=tuple(out_specs),
        scratch_shapes=[pltpu.VMEM((tm, d), BF16)],
        compiler_params=_params("parallel"),
        name="norm_proj",
    )(x2d, g.reshape(1, d), w_bf16)


def _out_proj_kernel(splits, *refs):
    n = len(splits)
    y_refs, h_ref, w_ref, o_ref = refs[:n], refs[n], refs[n + 1], refs[n + 2]
    acc = h_ref[...]
    for (start, width), y_ref in zip(splits, y_refs):
        acc = acc + _dot(y_ref[...].astype(BF16), w_ref[start:start + width, :])
    o_ref[...] = acc


def _out_proj(ys, time_major, h2d, w_bf16, batch, seq):
    m, d = h2d.shape
    tm = OUT_ROW_BLOCK
    n_l = seq // tm
    splits, in_specs, start = [], [], 0
    for y, tmaj in zip(ys, time_major):
        width = y.shape[1] // batch if tmaj else y.shape[1]
        splits.append((start, width))
        start += width
        if tmaj:
            in_specs.append(pl.BlockSpec((tm, width), lambda i: (i % n_l, i // n_l)))
        else:
            in_specs.append(pl.BlockSpec((tm, width), lambda i: (i, 0)))
    in_specs += [pl.BlockSpec((tm, d), lambda i: (i, 0)), _vmem_spec()]
    return pl.pallas_call(
        functools.partial(_out_proj_kernel, tuple(splits)),
        out_shape=jax.ShapeDtypeStruct((m, d), F32),
        grid=(m // tm,),
        in_specs=in_specs,
        out_specs=pl.BlockSpec((tm, d), lambda i: (i, 0)),
        compiler_params=_params("parallel"),
        name="out_proj",
    )(*ys, h2d, w_bf16)


def _ffn_kernel(h_ref, g_ref, w1_ref, w3_ref, w2_ref, o_ref, xn_ref, acc_ref):
    j = pl.program_id(1)

    @pl.when(j == 0)
    def _():
        xn_ref[...] = _rmsnorm_rows(h_ref[...], g_ref[...]).astype(BF16)
        acc_ref[...] = jnp.zeros_like(acc_ref)

    xn = xn_ref[...]
    a = _dot(xn, w1_ref[...])
    b = _dot(xn, w3_ref[...])
    act = (_silu(a) * b).astype(BF16)
    acc_ref[...] += _dot(act, w2_ref[...])

    @pl.when(j == pl.num_programs(1) - 1)
    def _():
        o_ref[...] = h_ref[...] + acc_ref[...]


def _ffn(h2d, g, w1, w3, w2):
    m, d = h2d.shape
    hidden = w1.shape[1]
    tm, th = FFN_ROW_BLOCK, FFN_HIDDEN_BLOCK
    return pl.pallas_call(
        _ffn_kernel,
        out_shape=jax.ShapeDtypeStruct((m, d), F32),
        grid=(m // tm, hidden // th),
        in_specs=[pl.BlockSpec((tm, d), lambda i, j: (i, 0)),
                  pl.BlockSpec((1, d), lambda i, j: (0, 0)),
                  pl.BlockSpec((d, th), lambda i, j: (0, j)),
                  pl.BlockSpec((d, th), lambda i, j: (0, j)),
                  pl.BlockSpec((th, d), lambda i, j: (j, 0))],
        out_specs=pl.BlockSpec((tm, d), lambda i, j: (i, 0)),
        scratch_shapes=[pltpu.VMEM((tm, d), BF16), pltpu.VMEM((tm, d), F32)],
        compiler_params=_params("parallel", "arbitrary"),
        name="ffn",
    )(h2d, g.reshape(1, d), w1, w3, w2)


def _s5_discretize_kernel(are_ref, aim_ref, logdt_ref, bre_ref, bim_ref,
                          abre_ref, abim_ref, bcre_ref, bcim_ref):
    a_re, a_im = are_ref[...], aim_ref[...]
    dt = jnp.exp(logdt_ref[...])
    mag = jnp.exp(a_re * dt)
    ab_re = mag * jnp.cos(a_im * dt)
    ab_im = mag * jnp.sin(a_im * dt)
    e_re, e_im = ab_re - 1.0, ab_im
    den = a_re * a_re + a_im * a_im
    coef_re = (e_re * a_re + e_im * a_im) / den
    coef_im = (e_im * a_re - e_re * a_im) / den
    abre_ref[...] = ab_re
    abim_ref[...] = ab_im
    b_re, b_im = bre_ref[...], bim_ref[...]
    c_re, c_im = coef_re[:, None, :], coef_im[:, None, :]
    bcre_ref[...] = c_re * b_re - c_im * b_im
    bcim_ref[...] = c_re * b_im + c_im * b_re


def _s5_discretize(a_re, a_im, log_dt, b_re, b_im):
    g, n = a_re.shape
    c = b_re.shape[2]
    b_re_t = jnp.transpose(b_re, (0, 2, 1))
    b_im_t = jnp.transpose(b_im, (0, 2, 1))
    return pl.pallas_call(
        _s5_discretize_kernel,
        out_shape=(jax.ShapeDtypeStruct((g, n), F32), jax.ShapeDtypeStruct((g, n), F32),
                   jax.ShapeDtypeStruct((g, c, n), F32), jax.ShapeDtypeStruct((g, c, n), F32)),
        name="s5_discretize",
    )(a_re, a_im, log_dt.reshape(g, 1), b_re_t, b_im_t)


def _block_diag_halves(blocks):
    g, r, c = blocks.shape
    hg = g // 2
    eye = jnp.eye(hg, dtype=blocks.dtype)
    b = blocks.reshape(2, hg, r, c)
    return jnp.einsum('hgrc,gk->hgrkc', b, eye).reshape(2, hg * r, hg * c)


def _s5_kernel(u_ref, wbre_ref, wbim_ref, abre_ref, abim_ref, wcre_ref, wcim_ref,
               d_ref, gw_ref, gb_ref, o_ref, xre_ref, xim_ref, hre_ref, him_ref):
    @pl.when(pl.program_id(0) == 0)
    def _():
        hre_ref[...] = jnp.zeros_like(hre_ref)
        him_ref[...] = jnp.zeros_like(him_ref)

    rows = u_ref.shape[0]
    n_steps = rows // SUBLANES
    u = u_ref[...]
    u_bf = u.astype(BF16)
    hs = S5_HALF_STATES
    y_halves = []
    for half in range(2):
        uh = u_bf[:, half * S5_HALF_CH:(half + 1) * S5_HALF_CH]
        xre_ref[...] = _dot(uh, wbre_ref[half])
        xim_ref[...] = _dot(uh, wbim_ref[half])
        a_re = jnp.broadcast_to(abre_ref[:, half * hs:(half + 1) * hs], (SUBLANES, hs))
        a_im = jnp.broadcast_to(abim_ref[:, half * hs:(half + 1) * hs], (SUBLANES, hs))

        def step(t, carry):
            h_re, h_im = carry
            r0 = pl.multiple_of(t * SUBLANES, SUBLANES)
            n_re = a_re * h_re - a_im * h_im + xre_ref[pl.ds(r0, SUBLANES), :]
            n_im = a_re * h_im + a_im * h_re + xim_ref[pl.ds(r0, SUBLANES), :]
            xre_ref[pl.ds(r0, SUBLANES), :] = n_re
            xim_ref[pl.ds(r0, SUBLANES), :] = n_im
            return n_re, n_im

        h0 = (hre_ref[:, half * hs:(half + 1) * hs], him_ref[:, half * hs:(half + 1) * hs])
        h_re, h_im = lax.fori_loop(0, n_steps, step, h0, unroll=4)
        hre_ref[:, half * hs:(half + 1) * hs] = h_re
        him_ref[:, half * hs:(half + 1) * hs] = h_im
        y_halves.append(_dot(xre_ref[...].astype(BF16), wcre_ref[half])
                        - _dot(xim_ref[...].astype(BF16), wcim_ref[half]))
    y = jnp.concatenate(y_halves, axis=-1) + d_ref[...] * u
    y = _gelu_tanh(y)
    gate = _sigmoid(_dot(y.astype(BF16), gw_ref[...]) + gb_ref[...])
    o_ref[...] = y * gate


def _s5_mixer(u_tb, batch, seq, wb_re, wb_im, ab_re, ab_im, wc_re, wc_im, d_skip, glu_w, glu_b):
    rows = S5_TIME_BLOCK * batch
    width = S5_WIDTH
    return pl.pallas_call(
        _s5_kernel,
        out_shape=jax.ShapeDtypeStruct((seq * batch, width), F32),
        grid=(seq // S5_TIME_BLOCK,),
        in_specs=[pl.BlockSpec((rows, width), lambda i: (i, 0))] + [_vmem_spec()] * 9,
        out_specs=pl.BlockSpec((rows, width), lambda i: (i, 0)),
        scratch_shapes=[pltpu.VMEM((rows, S5_HALF_STATES), F32),
                        pltpu.VMEM((rows, S5_HALF_STATES), F32),
                        pltpu.VMEM((batch, 2 * S5_HALF_STATES), F32),
                        pltpu.VMEM((batch, 2 * S5_HALF_STATES), F32)],
        compiler_params=_params("arbitrary"),
        name="s5_mixer",
    )(u_tb, wb_re, wb_im, ab_re, ab_im, wc_re, wc_im, d_skip, glu_w, glu_b)


def _ssd_kernel(z_ref, xbc_ref, dtraw_ref, convw_ref, convb_ref, dtb_ref, alog_ref, dexp_ref,
                normw_ref, tril_ref, expand_ref, o_ref, pad_ref, state_ref):
    q = SSD_CHUNK
    p = SSD_HEADDIM

    @pl.when(pl.program_id(1) == 0)
    def _():
        pad_ref[0:SUBLANES, :] = jnp.zeros((SUBLANES, SSD_CONV_CH), F32)
        state_ref[...] = jnp.zeros_like(state_ref)

    pad_ref[SUBLANES:SUBLANES + q, :] = xbc_ref[...]
    conv = convb_ref[...] + convw_ref[0:1, :] * pad_ref[SUBLANES - 3:SUBLANES - 3 + q, :]
    for k in range(1, SSD_CONV):
        off = SUBLANES - (SSD_CONV - 1) + k
        conv = conv + convw_ref[k:k + 1, :] * pad_ref[off:off + q, :]
    pad_ref[0:SUBLANES, :] = pad_ref[q:q + SUBLANES, :]
    act = _silu(conv)
    xs = act[:, :SSD_WIDTH]
    b_bf = act[:, SSD_WIDTH:SSD_WIDTH + SSD_BC_WIDTH].astype(BF16)
    c_bf = act[:, SSD_WIDTH + SSD_BC_WIDTH:].astype(BF16)

    dt = _softplus(dtraw_ref[...] + dtb_ref[...])
    dt_a = dt * (-jnp.exp(alog_ref[...]))
    cs = _dot_exact_lhs01(tril_ref[...], dt_a)
    cs_t = cs.T
    expand = expand_ref[...]
    cs_e = _dot_exact_rhs01(cs, expand)
    dt_e = _dot_exact_rhs01(dt, expand)
    xdt = xs * dt_e
    cs_last_e = cs_e[q - 1:q, :]
    decay_in = jnp.exp(cs_e)
    chunk_decay = jnp.exp(cs_last_e)
    x_end = (xdt * jnp.exp(cs_last_e - cs_e)).astype(BF16)

    row_i = lax.broadcasted_iota(jnp.int32, (q, q), 0)
    col_i = lax.broadcasted_iota(jnp.int32, (q, q), 1)
    causal = row_i >= col_i
    lane = lax.broadcasted_iota(jnp.int32, (q, 2 * p), 1)

    y_groups = []
    for g in range(SSD_GROUPS):
        bg = b_bf[:, g * SSD_STATE:(g + 1) * SSD_STATE]
        cg = c_bf[:, g * SSD_STATE:(g + 1) * SSD_STATE]
        gsl = slice(g * SSD_GROUP_WIDTH, (g + 1) * SSD_GROUP_WIDTH)
        cb = _dot_nt(cg, bg)
        state = state_ref[g]
        y_off = _dot(cg, state.astype(BF16)) * decay_in[:, gsl]
        y_pairs = []
        for pr in range(SSD_HPG // 2):
            h0 = g * SSD_HPG + 2 * pr
            xp = xdt[:, h0 * p:(h0 + 2) * p]
            acc = None
            for k in range(2):
                h = h0 + k
                seg = jnp.broadcast_to(cs[:, h:h + 1], (q, q)) - jnp.broadcast_to(cs_t[h:h + 1, :], (q, q))
                decay = jnp.exp(jnp.where(causal, seg, NEG_BIG))
                m = (cb * decay).astype(BF16)
                in_head = (lane >= k * p) & (lane < (k + 1) * p)
                xh = jnp.where(in_head, xp, 0.0).astype(BF16)
                part = _dot(m, xh)
                acc = part if acc is None else acc + part
            y_pairs.append(acc)
        y_groups.append(jnp.concatenate(y_pairs, axis=-1) + y_off)
        new_state = _dot_tn(bg, x_end[:, gsl])
        state_ref[g] = state * chunk_decay[:, gsl] + new_state

    y = jnp.concatenate(y_groups, axis=-1) + dexp_ref[...] * xs
    y = y * _silu(z_ref[...])
    outs = []
    for g in range(SSD_GROUPS):
        gsl = slice(g * SSD_GROUP_WIDTH, (g + 1) * SSD_GROUP_WIDTH)
        yg = y[:, gsl]
        ms = jnp.mean(yg * yg, axis=-1, keepdims=True)
        outs.append(yg * lax.rsqrt(ms + RMS_EPS) * normw_ref[:, gsl])
    o_ref[...] = jnp.concatenate(outs, axis=-1)


def _ssd_mixer(z, xbc, dt_raw, batch, seq, conv_w, conv_b, dt_bias, a_log, d_skip, norm_w):
    q = SSD_CHUNK
    n_c = seq // q
    heads_pad = LANES
    pad_h = heads_pad - SSD_HEADS
    dtb = jnp.pad(dt_bias, (0, pad_h)).reshape(1, heads_pad)
    alog = jnp.pad(a_log, (0, pad_h)).reshape(1, heads_pad)
    d_exp = jnp.repeat(d_skip, SSD_HEADDIM).reshape(1, SSD_WIDTH)
    tril = jnp.tril(jnp.ones((q, q), F32)).astype(BF16)
    expand = (jnp.arange(heads_pad)[:, None] == (jnp.arange(SSD_WIDTH) // SSD_HEADDIM)[None, :]).astype(BF16)
    row_map = lambda b, c: (b * n_c + c, 0)
    return pl.pallas_call(
        _ssd_kernel,
        out_shape=jax.ShapeDtypeStruct((batch * seq, SSD_WIDTH), F32),
        grid=(batch, n_c),
        in_specs=[pl.BlockSpec((q, SSD_WIDTH), row_map),
                  pl.BlockSpec((q, SSD_CONV_CH), row_map),
                  pl.BlockSpec((q, heads_pad), row_map)] + [_vmem_spec()] * 8,
        out_specs=pl.BlockSpec((q, SSD_WIDTH), row_map),
        scratch_shapes=[pltpu.VMEM((q + SUBLANES, SSD_CONV_CH), F32),
                        pltpu.VMEM((SSD_GROUPS, SSD_STATE, SSD_GROUP_WIDTH), F32)],
        compiler_params=_params("parallel", "arbitrary"),
        name="ssd_mixer",
    )(z, xbc, dt_raw, conv_w, conv_b.reshape(1, SSD_CONV_CH), dtb, alog, d_exp,
      norm_w.reshape(1, SSD_WIDTH), tril, expand)


def _head_rmsnorm(x, w_tiled, seg_ones, head_dim):
    sq = x * x
    hi = sq.astype(BF16)
    lo = (sq - hi.astype(F32)).astype(BF16)
    width = x.shape[1]
    blk = seg_ones.shape[0]
    parts = []
    for c0 in range(0, width, blk):
        parts.append(_dot(hi[:, c0:c0 + blk], seg_ones) + _dot(lo[:, c0:c0 + blk], seg_ones))
    ss = parts[0] if len(parts) == 1 else jnp.concatenate(parts, axis=-1)
    return x * lax.rsqrt(ss * (1.0 / head_dim) + RMS_EPS) * w_tiled


def _swa_kernel(q_ref, k_ref, v_ref, kprev_ref, vprev_ref, qw_ref, kw_ref, sink_ref, seg_ref, rep_ref, o_ref):
    blk = SWA_WINDOW
    hd = SWA_HEADDIM
    slab = SWA_HPG * hd
    first = pl.program_id(1) == 0

    seg_ones = seg_ref[...]
    qn = _head_rmsnorm(q_ref[...], qw_ref[...], seg_ones, hd)
    k_both = jnp.concatenate([kprev_ref[...], k_ref[...]], axis=0)
    k_cat = _head_rmsnorm(k_both, kw_ref[...], seg_ones, hd).astype(BF16)
    v_cat = jnp.concatenate([vprev_ref[...], v_ref[...]], axis=0).astype(BF16)

    rows = SWA_HPG * blk
    qpos = lax.broadcasted_iota(jnp.int32, (rows, 2 * blk), 0) & (blk - 1)
    kpos = lax.broadcasted_iota(jnp.int32, (rows, 2 * blk), 1) - blk
    first_i = jnp.where(first, 1, 0)
    lower = (qpos - blk) * (1 - first_i) - first_i
    valid = (kpos <= qpos) & (kpos > lower)
    lane = lax.broadcasted_iota(jnp.int32, (blk, slab), 1)
    head_masks = [(lane >= r * hd) & (lane < (r + 1) * hd) for r in range(SWA_HPG)]
    scale = hd ** -0.5

    for g in range(SWA_KV_HEADS):
        rep = rep_ref[g]
        k_rep = _dot(k_cat, rep).astype(BF16)
        v_rep = _dot(v_cat, rep).astype(BF16)
        q_slab = qn[:, g * slab:(g + 1) * slab]
        q_stack = jnp.concatenate([jnp.where(mk, q_slab, 0.0) for mk in head_masks], axis=0).astype(BF16)
        s = jnp.where(valid, _dot_nt(q_stack, k_rep) * scale, NEG_BIG)
        sink = jnp.concatenate(
            [jnp.full((blk, 1), sink_ref[g * SWA_HPG + r], F32) for r in range(SWA_HPG)], axis=0)
        m = jnp.maximum(jnp.max(s, axis=-1, keepdims=True), sink)
        p = jnp.exp(s - m)
        denom = jnp.sum(p, axis=-1, keepdims=True) + jnp.exp(sink - m)
        o_stack = _dot((p * (1.0 / denom)).astype(BF16), v_rep)
        o_slab = jnp.where(head_masks[0], o_stack[0:blk], 0.0)
        for r in range(1, SWA_HPG):
            o_slab = o_slab + jnp.where(head_masks[r], o_stack[r * blk:(r + 1) * blk], 0.0)
        o_ref[:, g * slab:(g + 1) * slab] = o_slab


def _swa_mixer(q, k, v, batch, seq, q_norm_w, k_norm_w, sinks):
    blk = SWA_WINDOW
    n_b = seq // blk
    qw = jnp.tile(q_norm_w, SWA_HEADS).reshape(1, SWA_WIDTH)
    kw = jnp.tile(k_norm_w, SWA_KV_HEADS).reshape(1, SWA_KV_WIDTH)
    seg = SWA_KV_WIDTH
    lane = jnp.arange(seg)
    seg_ones = (lane[:, None] // SWA_HEADDIM == lane[None, :] // SWA_HEADDIM).astype(BF16)
    rep = (lane[None, :, None] == (jnp.arange(SWA_KV_HEADS)[:, None, None] * SWA_HEADDIM
                                   + lane[None, None, :] % SWA_HEADDIM)).astype(BF16)
    row_map = lambda b, i: (b * n_b + i, 0)
    prev_map = lambda b, i: (b * n_b + jnp.maximum(i - 1, 0), 0)
    return pl.pallas_call(
        _swa_kernel,
        out_shape=jax.ShapeDtypeStruct((batch * seq, SWA_WIDTH), F32),
        grid=(batch, n_b),
        in_specs=[pl.BlockSpec((blk, SWA_WIDTH), row_map),
                  pl.BlockSpec((blk, SWA_KV_WIDTH), row_map),
                  pl.BlockSpec((blk, SWA_KV_WIDTH), row_map),
                  pl.BlockSpec((blk, SWA_KV_WIDTH), prev_map),
                  pl.BlockSpec((blk, SWA_KV_WIDTH), prev_map),
                  _vmem_spec(), _vmem_spec(),
                  pl.BlockSpec(memory_space=pltpu.SMEM),
                  _vmem_spec(), _vmem_spec()],
        out_specs=pl.BlockSpec((blk, SWA_WIDTH), row_map),
        compiler_params=_params("parallel", "parallel"),
        name="swa_mixer",
    )(q, k, v, k, v, qw, kw, sinks, seg_ones, rep)


def _gla_kernel(q_ref, k_ref, v_ref, gout_ref, glr_ref, w2_ref, gb_ref, normw_ref, tril_ref,
                o_ref, state_ref):
    c = GLA_CHUNK
    dk, dv = GLA_HEAD_K, GLA_HEAD_V
    rows = q_ref.shape[0]

    @pl.when(pl.program_id(1) == 0)
    def _():
        state_ref[...] = jnp.zeros_like(state_ref)

    w2 = w2_ref[...]
    g_hi, g_mid, g_lo = _split3(glr_ref[...])
    w_hi, w_mid, w_lo = _split3(w2)
    pre = (_dot(g_hi, w_hi) + _dot(g_hi, w_mid) + _dot(g_mid, w_hi)
           + _dot(g_hi, w_lo) + _dot(g_mid, w_mid) + _dot(g_lo, w_hi)) + gb_ref[...]
    gk = -_softplus(-pre) * (1.0 / GLA_GATE_NORMALIZER)
    bc = _dot_exact_lhs01(tril_ref[...], gk)

    row_i = lax.broadcasted_iota(jnp.int32, (c, c), 0)
    col_i = lax.broadcasted_iota(jnp.int32, (c, c), 1)
    causal = row_i >= col_i
    scale = dk ** -0.5
    q_all, k_all, v_all = q_ref[...], k_ref[...], v_ref[...]

    out_rows = []
    for ci in range(rows // c):
        rsl = slice(ci * c, (ci + 1) * c)
        heads = []
        for h in range(GLA_HEADS):
            ksl = slice(h * dk, (h + 1) * dk)
            vsl = slice(h * dv, (h + 1) * dv)
            bch = bc[rsl, ksl]
            b_last = bch[c - 1:c, :]
            qh = q_all[rsl, ksl] * scale
            kh = k_all[rsl, ksl]
            vh = v_all[rsl, vsl].astype(BF16)
            q_t = (qh * jnp.exp(bch)).astype(BF16)
            k_t = (kh * jnp.exp(-bch)).astype(BF16)
            att = jnp.where(causal, _dot_nt(q_t, k_t), 0.0).astype(BF16)
            state_t = state_ref[h]
            o = _dot(att, vh) + _dot_nt(q_t, state_t.astype(BF16))
            k_end = (kh * jnp.exp(b_last - bch)).astype(BF16)
            state_ref[h] = state_t * jnp.exp(b_last) + _dot_tn(vh, k_end)
            ms = jnp.mean(o * o, axis=-1, keepdims=True)
            heads.append(o * lax.rsqrt(ms + RMS_EPS) * normw_ref[...])
        out_rows.append(jnp.concatenate(heads, axis=-1))
    o_all = jnp.concatenate(out_rows, axis=0)
    o_ref[...] = o_all * _silu(gout_ref[...])


def _gla_mixer(q, k, v, gout, g_lr, batch, seq, gate_w2, gate_b, norm_w):
    tq = GLA_TIME_BLOCK
    n_t = seq // tq
    rank_pad = LANES
    w2 = jnp.pad(gate_w2, ((0, rank_pad - GLA_GATE_RANK), (0, 0)))
    idx = jnp.arange(tq)
    tril = ((idx[:, None] >= idx[None, :]) & (idx[:, None] // GLA_CHUNK == idx[None, :] // GLA_CHUNK)).astype(BF16)
    row_map = lambda b, i: (b * n_t + i, 0)
    return pl.pallas_call(
        _gla_kernel,
        out_shape=jax.ShapeDtypeStruct((batch * seq, GLA_VALUE_WIDTH), F32),
        grid=(batch, n_t),
        in_specs=[pl.BlockSpec((tq, GLA_KEY_WIDTH), row_map),
                  pl.BlockSpec((tq, GLA_KEY_WIDTH), row_map),
                  pl.BlockSpec((tq, GLA_VALUE_WIDTH), row_map),
                  pl.BlockSpec((tq, GLA_VALUE_WIDTH), row_map),
                  pl.BlockSpec((tq, rank_pad), row_map)] + [_vmem_spec()] * 4,
        out_specs=pl.BlockSpec((tq, GLA_VALUE_WIDTH), row_map),
        scratch_shapes=[pltpu.VMEM((GLA_HEADS, GLA_HEAD_V, GLA_HEAD_K), F32)],
        compiler_params=_params("parallel", "arbitrary"),
        name="gla_mixer",
    )(q, k, v, gout, g_lr, w2, gate_b.reshape(1, GLA_KEY_WIDTH), norm_w.reshape(1, GLA_HEAD_V), tril)


def _pad_cols(w, total):
    return jnp.pad(w, ((0, 0), (0, total - w.shape[1])))


def _layer0(h2d, batch, seq, norm_mix, w_in, s5_A_re, s5_A_im, s5_log_dt, s5_B_re, s5_B_im, s5_C_re,
            s5_C_im, s5_D, s5_glu_w, s5_glu_b, ssd_conv_w, ssd_conv_b, ssd_dt_bias, ssd_A_log,
            ssd_D, ssd_norm_w, w_out, norm_ffn, w1, w3, w2):
    pieces = [(0, S5_WIDTH), (S5_WIDTH, SSD_WIDTH), (S5_WIDTH + SSD_WIDTH, SSD_CONV_CH),
              (S5_WIDTH + SSD_WIDTH + SSD_CONV_CH, LANES)]
    w_in_p = _pad_cols(w_in, pieces[-1][0] + LANES).astype(BF16)
    u, z, xbc, dt_raw = _norm_proj(h2d, norm_mix, w_in_p, pieces, 0, batch, seq)

    ab_re, ab_im, bc_re, bc_im = _s5_discretize(s5_A_re, s5_A_im, s5_log_dt, s5_B_re, s5_B_im)
    wb_re = _block_diag_halves(bc_re).astype(BF16)
    wb_im = _block_diag_halves(bc_im).astype(BF16)
    wc_re = _block_diag_halves(jnp.transpose(s5_C_re, (0, 2, 1))).astype(BF16)
    wc_im = _block_diag_halves(jnp.transpose(s5_C_im, (0, 2, 1))).astype(BF16)
    ya = _s5_mixer(u.reshape(seq * batch, S5_WIDTH), batch, seq, wb_re, wb_im,
                   ab_re.reshape(1, -1), ab_im.reshape(1, -1), wc_re, wc_im,
                   s5_D.reshape(1, S5_WIDTH), s5_glu_w.astype(BF16), s5_glu_b.reshape(1, S5_WIDTH))
    ya = ya.reshape(seq, batch * S5_WIDTH)

    yb = _ssd_mixer(z, xbc, dt_raw, batch, seq, ssd_conv_w, ssd_conv_b, ssd_dt_bias, ssd_A_log,
                    ssd_D, ssd_norm_w)
    h2d = _out_proj([ya, yb], [True, False], h2d, w_out.astype(BF16), batch, seq)
    return _ffn(h2d, norm_ffn, w1.astype(BF16), w3.astype(BF16), w2.astype(BF16))


def _layer1(h2d, batch, seq, norm_mix, w_in, swa_q_norm, swa_k_norm, swa_sinks, gla_gate_w2,
            gla_gate_b, gla_norm_w, w_out, norm_ffn, w1, w3, w2):
    widths = (SWA_WIDTH, SWA_KV_WIDTH, SWA_KV_WIDTH, GLA_KEY_WIDTH, GLA_KEY_WIDTH,
              GLA_VALUE_WIDTH, GLA_VALUE_WIDTH, LANES)
    starts = np.concatenate([[0], np.cumsum(widths)[:-1]]).tolist()
    pieces = list(zip(starts, widths))
    w_in_p = _pad_cols(w_in, starts[-1] + LANES).astype(BF16)
    qc, kc, vc, qd, kd, vd, gout, g_lr = _norm_proj(h2d, norm_mix, w_in_p, pieces, None, batch, seq)
    yc = _swa_mixer(qc, kc, vc, batch, seq, swa_q_norm, swa_k_norm, swa_sinks)
    yd = _gla_mixer(qd, kd, vd, gout, g_lr, batch, seq, gla_gate_w2, gla_gate_b, gla_norm_w)
    h2d = _out_proj([yc, yd], [False, False], h2d, w_out.astype(BF16), batch, seq)
    return _ffn(h2d, norm_ffn, w1.astype(BF16), w3.astype(BF16), w2.astype(BF16))


def kernel(x, norm0_mix, w_in0, s5_A_re, s5_A_im, s5_log_dt, s5_B_re, s5_B_im, s5_C_re, s5_C_im, s5_D, s5_glu_w, s5_glu_b, ssd_conv_w, ssd_conv_b, ssd_dt_bias, ssd_A_log, ssd_D, ssd_norm_w, w_out0, norm0_ffn, ffn0_w1, ffn0_w3, ffn0_w2, norm1_mix, w_in1, swa_q_norm, swa_k_norm, swa_sinks, gla_gate_w2, gla_gate_b, gla_norm_w, w_out1, norm1_ffn, ffn1_w1, ffn1_w3, ffn1_w2):
    batch, seq, d = x.shape
    assert batch == SUBLANES and d == D_MODEL and seq % OUT_ROW_BLOCK == 0
    h2d = x.reshape(batch * seq, d)
    h2d = _layer0(h2d, batch, seq, norm0_mix, w_in0, s5_A_re, s5_A_im, s5_log_dt, s5_B_re, s5_B_im,
                  s5_C_re, s5_C_im, s5_D, s5_glu_w, s5_glu_b, ssd_conv_w, ssd_conv_b, ssd_dt_bias,
                  ssd_A_log, ssd_D, ssd_norm_w, w_out0, norm0_ffn, ffn0_w1, ffn0_w3, ffn0_w2)
    h2d = _layer1(h2d, batch, seq, norm1_mix, w_in1, swa_q_norm, swa_k_norm, swa_sinks, gla_gate_w2,
                  gla_gate_b, gla_norm_w, w_out1, norm1_ffn, ffn1_w1, ffn1_w3, ffn1_w2)
    return h2d.reshape(batch, seq, d)
```

```python
import functools

import numpy as np
import jax
import jax.numpy as jnp
from jax import lax
from jax.experimental import pallas as pl
from jax.experimental.pallas import tpu as pltpu

F32 = jnp.float32
BF16 = jnp.bfloat16

LANES = 128
SUBLANES = 8
VMEM_LIMIT_BYTES = 56 * 1024 * 1024

D_MODEL = 2048
RMS_EPS = 1e-6
MIX_WIDTH = D_MODEL

S5_WIDTH = D_MODEL // 4
S5_GROUP_CH = 16
S5_GROUPS = S5_WIDTH // S5_GROUP_CH
S5_STATE = 64
S5_HALF_CH = S5_WIDTH // 2
S5_HALF_GROUPS = S5_GROUPS // 2
S5_HALF_STATES = S5_HALF_GROUPS * S5_STATE
S5_TIME_BLOCK = 128

SSD_WIDTH = MIX_WIDTH - S5_WIDTH
SSD_HEADDIM = 64
SSD_HEADS = SSD_WIDTH // SSD_HEADDIM
SSD_GROUPS = 4
SSD_HPG = SSD_HEADS // SSD_GROUPS
SSD_STATE = 128
SSD_CONV = 4
SSD_CHUNK = 128
SSD_BC_WIDTH = SSD_GROUPS * SSD_STATE
SSD_CONV_CH = SSD_WIDTH + 2 * SSD_BC_WIDTH
SSD_GROUP_WIDTH = SSD_WIDTH // SSD_GROUPS

SWA_HEADDIM = 64
SWA_WIDTH = MIX_WIDTH // 2
SWA_HEADS = SWA_WIDTH // SWA_HEADDIM
SWA_KV_HEADS = 4
SWA_HPG = SWA_HEADS // SWA_KV_HEADS
SWA_WINDOW = 128
SWA_KV_WIDTH = SWA_KV_HEADS * SWA_HEADDIM

GLA_VALUE_WIDTH = MIX_WIDTH - SWA_WIDTH
GLA_KEY_WIDTH = GLA_VALUE_WIDTH // 2
GLA_HEADS = 4
GLA_HEAD_K = GLA_KEY_WIDTH // GLA_HEADS
GLA_HEAD_V = GLA_VALUE_WIDTH // GLA_HEADS
GLA_GATE_RANK = 16
GLA_GATE_NORMALIZER = 16.0
GLA_CHUNK = 64
GLA_TIME_BLOCK = 128

FFN_HIDDEN = -(-(8 * D_MODEL) // (3 * 256)) * 256

PROJ_ROW_BLOCK = 256
FFN_ROW_BLOCK = 512
FFN_HIDDEN_BLOCK = 512
OUT_ROW_BLOCK = 512
PROJ_COL_CHUNK = 512

NEG_BIG = -1e30


def _params(*semantics):
    return pltpu.CompilerParams(dimension_semantics=semantics, vmem_limit_bytes=VMEM_LIMIT_BYTES)


def _vmem_spec():
    return pl.BlockSpec(memory_space=pltpu.VMEM)


def _dot(a, b):
    return jnp.dot(a, b, preferred_element_type=F32)


def _dot_nt(a, b):
    return lax.dot_general(a, b, (((1,), (1,)), ((), ())), preferred_element_type=F32)


def _dot_tn(a, b):
    return lax.dot_general(a, b, (((0,), (0,)), ((), ())), preferred_element_type=F32)


def _split3(v):
    hi = v.astype(BF16)
    r1 = v - hi.astype(F32)
    mid = r1.astype(BF16)
    lo = (r1 - mid.astype(F32)).astype(BF16)
    return hi, mid, lo


def _dot_exact_rhs01(v, m01):
    hi, mid, lo = _split3(v)
    return _dot(hi, m01) + _dot(mid, m01) + _dot(lo, m01)


def _dot_exact_lhs01(m01, v):
    hi, mid, lo = _split3(v)
    return _dot(m01, hi) + _dot(m01, mid) + _dot(m01, lo)


LOG2E = float(np.log2(np.e))


def _sigmoid(x):
    return 1.0 / (1.0 + jnp.exp2(x * (-LOG2E)))


def _silu(x):
    return x * _sigmoid(x)


def _softplus(x):
    return jnp.maximum(x, 0.0) + jnp.log1p(jnp.exp(-jnp.abs(x)))


def _gelu_tanh(x):
    c = np.sqrt(2.0 / np.pi).astype(np.float32)
    return 0.5 * x * (1.0 + jnp.tanh(c * (x + 0.044715 * (x * x * x))))


def _rmsnorm_rows(x, w):
    ms = jnp.mean(x * x, axis=-1, keepdims=True)
    return x * lax.rsqrt(ms + RMS_EPS) * w


def _norm_proj_kernel(pieces, x_ref, g_ref, w_ref, *rest):
    out_refs = rest[:len(pieces)]
    xn_ref = rest[len(pieces)]
    xn_ref[...] = _rmsnorm_rows(x_ref[...], g_ref[...]).astype(BF16)
    for (start, width), o_ref in zip(pieces, out_refs):
        for c0 in range(0, width, PROJ_COL_CHUNK):
            cw = min(PROJ_COL_CHUNK, width - c0)
            o_ref[:, c0:c0 + cw] = _dot(xn_ref[...], w_ref[:, start + c0:start + c0 + cw])


def _norm_proj(x2d, g, w_bf16, pieces, time_major_piece, batch, seq):
    m, d = x2d.shape
    tm = PROJ_ROW_BLOCK
    n_l = seq // tm
    out_shapes, out_specs = [], []
    for idx, (_, width) in enumerate(pieces):
        if idx == time_major_piece:
            out_shapes.append(jax.ShapeDtypeStruct((seq, batch * width), F32))
            out_specs.append(pl.BlockSpec((tm, width), lambda i: (i % n_l, i // n_l)))
        else:
            out_shapes.append(jax.ShapeDtypeStruct((m, width), F32))
            out_specs.append(pl.BlockSpec((tm, width), lambda i: (i, 0)))
    return pl.pallas_call(
        functools.partial(_norm_proj_kernel, tuple(pieces)),
        out_shape=tuple(out_shapes),
        grid=(m // tm,),
        in_specs=[pl.BlockSpec((tm, d), lambda i: (i, 0)),
                  pl.BlockSpec((1, d), lambda i: (0, 0)),
                  _vmem_spec()],
        out_specs=tuple(out_specs),
        scratch_shapes=[pltpu.VMEM((tm, d), BF16)],
        compiler_params=_params("parallel"),
        name="norm_proj",
    )(x2d, g.reshape(1, d), w_bf16)


def _out_proj_kernel(splits, *refs):
    n = len(splits)
    y_refs, h_ref, w_ref, o_ref = refs[:n], refs[n], refs[n + 1], refs[n + 2]
    acc = h_ref[...]
    for (start, width), y_ref in zip(splits, y_refs):
        acc = acc + _dot(y_ref[...].astype(BF16), w_ref[start:start + width, :])
    o_ref[...] = acc


def _out_proj(ys, time_major, h2d, w_bf16, batch, seq):
    m, d = h2d.shape
    tm = OUT_ROW_BLOCK
    n_l = seq // tm
    splits, in_specs, start = [], [], 0
    for y, tmaj in zip(ys, time_major):
        width = y.shape[1] // batch if tmaj else y.shape[1]
        splits.append((start, width))
        start += width
        if tmaj:
            in_specs.append(pl.BlockSpec((tm, width), lambda i: (i % n_l, i // n_l)))
        else:
            in_specs.append(pl.BlockSpec((tm, width), lambda i: (i, 0)))
    in_specs += [pl.BlockSpec((tm, d), lambda i: (i, 0)), _vmem_spec()]
    return pl.pallas_call(
        functools.partial(_out_proj_kernel, tuple(splits)),
        out_shape=jax.ShapeDtypeStruct((m, d), F32),
        grid=(m // tm,),
        in_specs=in_specs,
        out_specs=pl.BlockSpec((tm, d), lambda i: (i, 0)),
        compiler_params=_params("parallel"),
        name="out_proj",
    )(*ys, h2d, w_bf16)


def _ffn_kernel(h_ref, g_ref, w1_ref, w3_ref, w2_ref, o_ref, xn_ref, acc_ref):
    j = pl.program_id(1)

    @pl.when(j == 0)
    def _():
        xn_ref[...] = _rmsnorm_rows(h_ref[...], g_ref[...]).astype(BF16)
        acc_ref[...] = jnp.zeros_like(acc_ref)

    xn = xn_ref[...]
    a = _dot(xn, w1_ref[...])
    b = _dot(xn, w3_ref[...])
    act = (_silu(a) * b).astype(BF16)
    acc_ref[...] += _dot(act, w2_ref[...])

    @pl.when(j == pl.num_programs(1) - 1)
    def _():
        o_ref[...] = h_ref[...] + acc_ref[...]


def _ffn(h2d, g, w1, w3, w2):
    m, d = h2d.shape
    hidden = w1.shape[1]
    tm, th = FFN_ROW_BLOCK, FFN_HIDDEN_BLOCK
    return pl.pallas_call(
        _ffn_kernel,
        out_shape=jax.ShapeDtypeStruct((m, d), F32),
        grid=(m // tm, hidden // th),
        in_specs=[pl.BlockSpec((tm, d), lambda i, j: (i, 0)),
                  pl.BlockSpec((1, d), lambda i, j: (0, 0)),
                  pl.BlockSpec((d, th), lambda i, j: (0, j)),
                  pl.BlockSpec((d, th), lambda i, j: (0, j)),
                  pl.BlockSpec((th, d), lambda i, j: (j, 0))],
        out_specs=pl.BlockSpec((tm, d), lambda i, j: (i, 0)),
        scratch_shapes=[pltpu.VMEM((tm, d), BF16), pltpu.VMEM((tm, d), F32)],
        compiler_params=_params("parallel", "arbitrary"),
        name="ffn",
    )(h2d, g.reshape(1, d), w1, w3, w2)


def _s5_discretize_kernel(are_ref, aim_ref, logdt_ref, bre_ref, bim_ref,
                          abre_ref, abim_ref, bcre_ref, bcim_ref):
    a_re, a_im = are_ref[...], aim_ref[...]
    dt = jnp.exp(logdt_ref[...])
    mag = jnp.exp(a_re * dt)
    ab_re = mag * jnp.cos(a_im * dt)
    ab_im = mag * jnp.sin(a_im * dt)
    e_re, e_im = ab_re - 1.0, ab_im
    den = a_re * a_re + a_im * a_im
    coef_re = (e_re * a_re + e_im * a_im) / den
    coef_im = (e_im * a_re - e_re * a_im) / den
    abre_ref[...] = ab_re
    abim_ref[...] = ab_im
    b_re, b_im = bre_ref[...], bim_ref[...]
    c_re, c_im = coef_re[:, None, :], coef_im[:, None, :]
    bcre_ref[...] = c_re * b_re - c_im * b_im
    bcim_ref[...] = c_re * b_im + c_im * b_re


def _s5_discretize(a_re, a_im, log_dt, b_re, b_im):
    g, n = a_re.shape
    c = b_re.shape[2]
    b_re_t = jnp.transpose(b_re, (0, 2, 1))
    b_im_t = jnp.transpose(b_im, (0, 2, 1))
    return pl.pallas_call(
        _s5_discretize_kernel,
        out_shape=(jax.ShapeDtypeStruct((g, n), F32), jax.ShapeDtypeStruct((g, n), F32),
                   jax.ShapeDtypeStruct((g, c, n), F32), jax.ShapeDtypeStruct((g, c, n), F32)),
        name="s5_discretize",
    )(a_re, a_im, log_dt.reshape(g, 1), b_re_t, b_im_t)


def _block_diag_halves(blocks):
    g, r, c = blocks.shape
    hg = g // 2
    eye = jnp.eye(hg, dtype=blocks.dtype)
    b = blocks.reshape(2, hg, r, c)
    return jnp.einsum('hgrc,gk->hgrkc', b, eye).reshape(2, hg * r, hg * c)


def _s5_kernel(u_ref, wbre_ref, wbim_ref, abre_ref, abim_ref, wcre_ref, wcim_ref,
               d_ref, gw_ref, gb_ref, o_ref, xre_ref, xim_ref, hre_ref, him_ref):
    @pl.when(pl.program_id(0) == 0)
    def _():
        hre_ref[...] = jnp.zeros_like(hre_ref)
        him_ref[...] = jnp.zeros_like(him_ref)

    rows = u_ref.shape[0]
    n_steps = rows // SUBLANES
    u = u_ref[...]
    u_bf = u.astype(BF16)
    hs = S5_HALF_STATES
    y_halves = []
    for half in range(2):
        uh = u_bf[:, half * S5_HALF_CH:(half + 1) * S5_HALF_CH]
        xre_ref[...] = _dot(uh, wbre_ref[half])
        xim_ref[...] = _dot(uh, wbim_ref[half])
        a_re = jnp.broadcast_to(abre_ref[:, half * hs:(half + 1) * hs], (SUBLANES, hs))
        a_im = jnp.broadcast_to(abim_ref[:, half * hs:(half + 1) * hs], (SUBLANES, hs))

        def step(t, carry):
            h_re, h_im = carry
            r0 = pl.multiple_of(t * SUBLANES, SUBLANES)
            n_re = a_re * h_re - a_im * h_im + xre_ref[pl.ds(r0, SUBLANES), :]
            n_im = a_re * h_im + a_im * h_re + xim_ref[pl.ds(r0, SUBLANES), :]
            xre_ref[pl.ds(r0, SUBLANES), :] = n_re
            xim_ref[pl.ds(r0, SUBLANES), :] = n_im
            return n_re, n_im

        h0 = (hre_ref[:, half * hs:(half + 1) * hs], him_ref[:, half * hs:(half + 1) * hs])
        h_re, h_im = lax.fori_loop(0, n_steps, step, h0, unroll=4)
        hre_ref[:, half * hs:(half + 1) * hs] = h_re
        him_ref[:, half * hs:(half + 1) * hs] = h_im
        y_halves.append(_dot(xre_ref[...].astype(BF16), wcre_ref[half])
                        - _dot(xim_ref[...].astype(BF16), wcim_ref[half]))
    y = jnp.concatenate(y_halves, axis=-1) + d_ref[...] * u
    y = _gelu_tanh(y)
    gate = _sigmoid(_dot(y.astype(BF16), gw_ref[...]) + gb_ref[...])
    o_ref[...] = y * gate


def _s5_mixer(u_tb, batch, seq, wb_re, wb_im, ab_re, ab_im, wc_re, wc_im, d_skip, glu_w, glu_b):
    rows = S5_TIME_BLOCK * batch
    width = S5_WIDTH
    return pl.pallas_call(
        _s5_kernel,
        out_shape=jax.ShapeDtypeStruct((seq * batch, width), F32),
        grid=(seq // S5_TIME_BLOCK,),
        in_specs=[pl.BlockSpec((rows, width), lambda i: (i, 0))] + [_vmem_spec()] * 9,
        out_specs=pl.BlockSpec((rows, width), lambda i: (i, 0)),
        scratch_shapes=[pltpu.VMEM((rows, S5_HALF_STATES), F32),
                        pltpu.VMEM((rows, S5_HALF_STATES), F32),
                        pltpu.VMEM((batch, 2 * S5_HALF_STATES), F32),
                        pltpu.VMEM((batch, 2 * S5_HALF_STATES), F32)],
        compiler_params=_params("arbitrary"),
        name="s5_mixer",
    )(u_tb, wb_re, wb_im, ab_re, ab_im, wc_re, wc_im, d_skip, glu_w, glu_b)


def _ssd_kernel(z_ref, xbc_ref, tail_ref, dtraw_ref, convw_ref, convb_ref, dtb_ref, alog_ref, dexp_ref,
                normw_ref, tril_ref, expand_ref, shift_ref, o_ref, state_ref):
    q = SSD_CHUNK
    p = SSD_HEADDIM
    first = pl.program_id(1) == 0

    @pl.when(first)
    def _():
        state_ref[...] = jnp.zeros_like(state_ref)

    x = xbc_ref[...]
    tail = tail_ref[...] * jnp.where(first, 0.0, 1.0)
    row8 = lax.broadcasted_iota(jnp.int32, (SUBLANES, SSD_CONV_CH), 0)
    conv = convb_ref[...] + convw_ref[SSD_CONV - 1:SSD_CONV, :] * x
    for k in range(SSD_CONV - 1):
        back = SSD_CONV - 1 - k
        rolled = pltpu.roll(x, back, 0)
        top = jnp.where(row8 < back, pltpu.roll(tail, back, 0), rolled[0:SUBLANES])
        shifted = jnp.concatenate([top, rolled[SUBLANES:]], axis=0)
        conv = conv + convw_ref[k:k + 1, :] * shifted
    act = _silu(conv)
    xs = act[:, :SSD_WIDTH]
    b_bf = act[:, SSD_WIDTH:SSD_WIDTH + SSD_BC_WIDTH].astype(BF16)
    c_bf = act[:, SSD_WIDTH + SSD_BC_WIDTH:].astype(BF16)

    dt = _softplus(dtraw_ref[...] + dtb_ref[...])
    dt_a = dt * (-jnp.exp(alog_ref[...]) * LOG2E)
    hp = dt.shape[1]
    cs3 = _dot(tril_ref[...], jnp.concatenate(_split3(dt_a), axis=1))
    cs = cs3[:, :hp] + cs3[:, hp:2 * hp] + cs3[:, 2 * hp:]
    cs_t = cs.T
    cs_e = _dot(jnp.concatenate(_split3(cs), axis=1), expand_ref[...])
    dt_hi = dt.astype(BF16)
    dt_lo = (dt - dt_hi.astype(F32)).astype(BF16)
    dt_e = _dot(jnp.concatenate([dt_hi, dt_lo], axis=1), expand_ref[0:2 * hp, :])
    xdt = xs * dt_e
    xdt_bf = xdt.astype(BF16)
    cs_last_e = cs_e[q - 1:q, :]
    decay_in = jnp.exp2(cs_e)
    chunk_decay = jnp.exp2(cs_last_e)
    x_end = (xdt * jnp.exp2(cs_last_e - cs_e)).astype(BF16)

    row_i = lax.broadcasted_iota(jnp.int32, (q, q), 0)
    col_i = lax.broadcasted_iota(jnp.int32, (q, q), 1)
    causal = row_i >= col_i
    first_of_pair = lax.broadcasted_iota(jnp.int32, (q, 2 * p), 1) < p

    y_groups = []
    for g in range(SSD_GROUPS):
        bg = b_bf[:, g * SSD_STATE:(g + 1) * SSD_STATE]
        cg = c_bf[:, g * SSD_STATE:(g + 1) * SSD_STATE]
        gsl = slice(g * SSD_GROUP_WIDTH, (g + 1) * SSD_GROUP_WIDTH)
        cb = _dot_nt(cg, bg)
        state = state_ref[g]
        y_off = _dot(cg, state.astype(BF16)) * decay_in[:, gsl]
        y_pairs = []
        for pr in range(SSD_HPG // 2):
            h0 = g * SSD_HPG + 2 * pr
            xp = xdt_bf[:, h0 * p:(h0 + 2) * p]
            parts = []
            for h in (h0, h0 + 1):
                seg = jnp.broadcast_to(cs[:, h:h + 1], (q, q)) - jnp.broadcast_to(cs_t[h:h + 1, :], (q, q))
                decay = jnp.exp2(jnp.where(causal, seg, NEG_BIG))
                parts.append(_dot((cb * decay).astype(BF16), xp))
            y_pairs.append(jnp.where(first_of_pair, parts[0], parts[1]))
        y_groups.append(jnp.concatenate(y_pairs, axis=-1) + y_off)
        new_state = _dot_tn(bg, x_end[:, gsl])
        state_ref[g] = state * chunk_decay[:, gsl] + new_state

    y = jnp.concatenate(y_groups, axis=-1) + dexp_ref[...] * xs
    y = y * _silu(z_ref[...])
    outs = []
    for g in range(SSD_GROUPS):
        gsl = slice(g * SSD_GROUP_WIDTH, (g + 1) * SSD_GROUP_WIDTH)
        yg = y[:, gsl]
        ms = jnp.mean(yg * yg, axis=-1, keepdims=True)
        outs.append(yg * lax.rsqrt(ms + RMS_EPS) * normw_ref[:, gsl])
    o_ref[...] = jnp.concatenate(outs, axis=-1)


def _ssd_mixer(z, xbc, dt_raw, batch, seq, conv_w, conv_b, dt_bias, a_log, d_skip, norm_w):
    q = SSD_CHUNK
    n_c = seq // q
    heads_pad = LANES
    pad_h = heads_pad - SSD_HEADS
    dtb = jnp.pad(dt_bias, (0, pad_h)).reshape(1, heads_pad)
    alog = jnp.pad(a_log, (0, pad_h)).reshape(1, heads_pad)
    d_exp = jnp.repeat(d_skip, SSD_HEADDIM).reshape(1, SSD_WIDTH)
    t = jnp.arange(q)
    tril = (t[:, None] >= t[None, :]).astype(BF16)
    expand = (jnp.arange(heads_pad)[:, None] == (jnp.arange(SSD_WIDTH) // SSD_HEADDIM)[None, :]).astype(BF16)
    expand3 = jnp.tile(expand, (3, 1))
    shift = jnp.stack([(t[None, :] == t[:, None] - (SSD_CONV - 1 - k)) for k in range(SSD_CONV - 1)]).astype(BF16)
    row_map = lambda b, c: (b * n_c + c, 0)
    tiles_per_chunk = q // SUBLANES
    tail_map = lambda b, c: (jnp.maximum((b * n_c + c) * tiles_per_chunk - 1, 0), 0)
    return pl.pallas_call(
        _ssd_kernel,
        out_shape=jax.ShapeDtypeStruct((batch * seq, SSD_WIDTH), F32),
        grid=(batch, n_c),
        in_specs=[pl.BlockSpec((q, SSD_WIDTH), row_map),
                  pl.BlockSpec((q, SSD_CONV_CH), row_map),
                  pl.BlockSpec((SUBLANES, SSD_CONV_CH), tail_map),
                  pl.BlockSpec((q, heads_pad), row_map)] + [_vmem_spec()] * 9,
        out_specs=pl.BlockSpec((q, SSD_WIDTH), row_map),
        scratch_shapes=[pltpu.VMEM((SSD_GROUPS, SSD_STATE, SSD_GROUP_WIDTH), F32)],
        compiler_params=_params("parallel", "arbitrary"),
        name="ssd_mixer",
    )(z, xbc, xbc, dt_raw, conv_w, conv_b.reshape(1, SSD_CONV_CH), dtb, alog, d_exp,
      norm_w.reshape(1, SSD_WIDTH), tril, expand3, shift)


def _head_rmsnorm(x, w_tiled, seg_ones, head_dim):
    sq = x * x
    hi = sq.astype(BF16)
    lo = (sq - hi.astype(F32)).astype(BF16)
    width = x.shape[1]
    blk = seg_ones.shape[0]
    parts = []
    for c0 in range(0, width, blk):
        parts.append(_dot(hi[:, c0:c0 + blk], seg_ones) + _dot(lo[:, c0:c0 + blk], seg_ones))
    ss = parts[0] if len(parts) == 1 else jnp.concatenate(parts, axis=-1)
    return x * lax.rsqrt(ss * (1.0 / head_dim) + RMS_EPS) * w_tiled


def _swa_kernel(q_ref, k_ref, v_ref, kprev_ref, vprev_ref, qw_ref, kw_ref, sink_ref, seg_ref, rep_ref, o_ref):
    blk = SWA_WINDOW
    hd = SWA_HEADDIM
    slab = SWA_HPG * hd
    first = pl.program_id(1) == 0

    seg_ones = seg_ref[...]
    qn = _head_rmsnorm(q_ref[...], qw_ref[...], seg_ones, hd) * (hd ** -0.5)
    k_both = jnp.concatenate([kprev_ref[...], k_ref[...]], axis=0)
    k_cat = _head_rmsnorm(k_both, kw_ref[...], seg_ones, hd).astype(BF16)
    v_cat = jnp.concatenate([vprev_ref[...], v_ref[...]], axis=0).astype(BF16)

    qpos = lax.broadcasted_iota(jnp.int32, (blk, blk), 0)
    kpos = lax.broadcasted_iota(jnp.int32, (blk, blk), 1)
    in_prev = kpos > qpos
    prev_bias = jnp.where(first, NEG_BIG, 0.0)
    lane = lax.broadcasted_iota(jnp.int32, (blk, slab), 1)
    head_masks = [(lane >= r * hd) & (lane < (r + 1) * hd) for r in range(SWA_HPG)]

    scores, v_reps = [], []
    for g in range(SWA_KV_HEADS):
        rep = rep_ref[g]
        k_rep = _dot(k_cat, rep).astype(BF16)
        v_reps.append(_dot(v_cat, rep).astype(BF16))
        q_slab = qn[:, g * slab:(g + 1) * slab]
        q_stack = jnp.concatenate([jnp.where(mk, q_slab, 0.0) for mk in head_masks], axis=0).astype(BF16)
        scores.append(_dot_nt(q_stack, k_rep))

    probs = []
    for g in range(SWA_KV_HEADS):
        p_heads = []
        for r in range(SWA_HPG):
            s = scores[g][r * blk:(r + 1) * blk, :]
            s_fold = jnp.where(in_prev, s[:, :blk] + prev_bias, s[:, blk:])
            sink = sink_ref[g * SWA_HPG + r]
            m = jnp.maximum(jnp.max(s_fold, axis=-1, keepdims=True), sink)
            p = jnp.exp(s_fold - m)
            denom = jnp.sum(p, axis=-1, keepdims=True) + jnp.exp(sink - m)
            pn = p * (1.0 / denom)
            p_heads.append(jnp.concatenate([jnp.where(in_prev, pn, 0.0).astype(BF16),
                                            jnp.where(in_prev, 0.0, pn).astype(BF16)], axis=1))
        probs.append(jnp.concatenate(p_heads, axis=0))

    for g in range(SWA_KV_HEADS):
        o_stack = _dot(probs[g], v_reps[g])
        o_slab = jnp.where(head_masks[0], o_stack[0:blk], 0.0)
        for r in range(1, SWA_HPG):
            o_slab = o_slab + jnp.where(head_masks[r], o_stack[r * blk:(r + 1) * blk], 0.0)
        o_ref[:, g * slab:(g + 1) * slab] = o_slab


def _swa_mixer(q, k, v, batch, seq, q_norm_w, k_norm_w, sinks):
    blk = SWA_WINDOW
    n_b = seq // blk
    qw = jnp.tile(q_norm_w, SWA_HEADS).reshape(1, SWA_WIDTH)
    kw = jnp.tile(k_norm_w, SWA_KV_HEADS).reshape(1, SWA_KV_WIDTH)
    seg = SWA_KV_WIDTH
    lane = jnp.arange(seg)
    seg_ones = (lane[:, None] // SWA_HEADDIM == lane[None, :] // SWA_HEADDIM).astype(BF16)
    rep = (lane[None, :, None] == (jnp.arange(SWA_KV_HEADS)[:, None, None] * SWA_HEADDIM
                                   + lane[None, None, :] % SWA_HEADDIM)).astype(BF16)
    row_map = lambda b, i: (b * n_b + i, 0)
    prev_map = lambda b, i: (b * n_b + jnp.maximum(i - 1, 0), 0)
    return pl.pallas_call(
        _swa_kernel,
        out_shape=jax.ShapeDtypeStruct((batch * seq, SWA_WIDTH), F32),
        grid=(batch, n_b),
        in_specs=[pl.BlockSpec((blk, SWA_WIDTH), row_map),
                  pl.BlockSpec((blk, SWA_KV_WIDTH), row_map),
                  pl.BlockSpec((blk, SWA_KV_WIDTH), row_map),
                  pl.BlockSpec((blk, SWA_KV_WIDTH), prev_map),
                  pl.BlockSpec((blk, SWA_KV_WIDTH), prev_map),
                  _vmem_spec(), _vmem_spec(),
                  pl.BlockSpec(memory_space=pltpu.SMEM),
                  _vmem_spec(), _vmem_spec()],
        out_specs=pl.BlockSpec((blk, SWA_WIDTH), row_map),
        compiler_params=_params("parallel", "parallel"),
        name="swa_mixer",
    )(q, k, v, k, v, qw, kw, sinks, seg_ones, rep)


def _gla_kernel(q_ref, k_ref, v_ref, gout_ref, glr_ref, w2_ref, gb_ref, normw_ref, tril_ref,
                o_ref, state_ref):
    c = GLA_CHUNK
    dk, dv = GLA_HEAD_K, GLA_HEAD_V
    rows = q_ref.shape[0]

    @pl.when(pl.program_id(1) == 0)
    def _():
        state_ref[...] = jnp.zeros_like(state_ref)

    w2 = w2_ref[...]
    g_hi, g_mid, g_lo = _split3(glr_ref[...])
    w_hi, w_mid, w_lo = _split3(w2)
    pre = (_dot(g_hi, w_hi) + _dot(g_hi, w_mid) + _dot(g_mid, w_hi)
           + _dot(g_hi, w_lo) + _dot(g_mid, w_mid) + _dot(g_lo, w_hi)) + gb_ref[...]
    gk = -_softplus(-pre) * (LOG2E / GLA_GATE_NORMALIZER)
    gk3 = _dot(tril_ref[...], jnp.concatenate(_split3(gk), axis=1))
    kw = gk.shape[1]
    bc = gk3[:, :kw] + gk3[:, kw:2 * kw] + gk3[:, 2 * kw:]

    assert rows == 2 * c
    b_last0, b_last1 = bc[c - 1:c, :], bc[rows - 1:rows, :]
    b_last = jnp.concatenate([jnp.broadcast_to(b_last0, (c, kw)), jnp.broadcast_to(b_last1, (c, kw))], axis=0)
    k_all = k_ref[...]
    q_t = (q_ref[...] * (dk ** -0.5) * jnp.exp2(bc)).astype(BF16)
    k_t = (k_all * jnp.exp2(-bc)).astype(BF16)
    k_end = (k_all * jnp.exp2(b_last - bc)).astype(BF16)
    dec0, dec1 = jnp.exp2(b_last0), jnp.exp2(b_last1)
    v_bf = v_ref[...].astype(BF16)
    row_i = lax.broadcasted_iota(jnp.int32, (rows, rows), 0)
    col_i = lax.broadcasted_iota(jnp.int32, (rows, rows), 1)
    same_chunk_causal = (row_i >= col_i) & (col_i >= (row_i // c) * c)

    heads = []
    for h in range(GLA_HEADS):
        ksl = slice(h * dk, (h + 1) * dk)
        vsl = slice(h * dv, (h + 1) * dv)
        att = jnp.where(same_chunk_causal, _dot_nt(q_t[:, ksl], k_t[:, ksl]), 0.0).astype(BF16)
        o_intra = _dot(att, v_bf[:, vsl])
        s0 = state_ref[h]
        s1 = s0 * dec0[:, ksl] + _dot_tn(v_bf[0:c, vsl], k_end[0:c, ksl])
        o_inter = jnp.concatenate([_dot_nt(q_t[0:c, ksl], s0.astype(BF16)),
                                   _dot_nt(q_t[c:rows, ksl], s1.astype(BF16))], axis=0)
        state_ref[h] = s1 * dec1[:, ksl] + _dot_tn(v_bf[c:rows, vsl], k_end[c:rows, ksl])
        o = o_intra + o_inter
        ms = jnp.mean(o * o, axis=-1, keepdims=True)
        heads.append(o * lax.rsqrt(ms + RMS_EPS) * normw_ref[...])
    o_ref[...] = jnp.concatenate(heads, axis=-1) * _silu(gout_ref[...])


def _gla_mixer(q, k, v, gout, g_lr, batch, seq, gate_w2, gate_b, norm_w):
    tq = GLA_TIME_BLOCK
    n_t = seq // tq
    rank_pad = LANES
    w2 = jnp.pad(gate_w2, ((0, rank_pad - GLA_GATE_RANK), (0, 0)))
    idx = jnp.arange(tq)
    tril = ((idx[:, None] >= idx[None, :]) & (idx[:, None] // GLA_CHUNK == idx[None, :] // GLA_CHUNK)).astype(BF16)
    row_map = lambda b, i: (b * n_t + i, 0)
    return pl.pallas_call(
        _gla_kernel,
        out_shape=jax.ShapeDtypeStruct((batch * seq, GLA_VALUE_WIDTH), F32),
        grid=(batch, n_t),
        in_specs=[pl.BlockSpec((tq, GLA_KEY_WIDTH), row_map),
                  pl.BlockSpec((tq, GLA_KEY_WIDTH), row_map),
                  pl.BlockSpec((tq, GLA_VALUE_WIDTH), row_map),
                  pl.BlockSpec((tq, GLA_VALUE_WIDTH), row_map),
                  pl.BlockSpec((tq, rank_pad), row_map)] + [_vmem_spec()] * 4,
        out_specs=pl.BlockSpec((tq, GLA_VALUE_WIDTH), row_map),
        scratch_shapes=[pltpu.VMEM((GLA_HEADS, GLA_HEAD_V, GLA_HEAD_K), F32)],
        compiler_params=_params("parallel", "arbitrary"),
        name="gla_mixer",
    )(q, k, v, gout, g_lr, w2, gate_b.reshape(1, GLA_KEY_WIDTH), norm_w.reshape(1, GLA_HEAD_V), tril)


def _pad_cols(w, total):
    return jnp.pad(w, ((0, 0), (0, total - w.shape[1])))


def _layer0(h2d, batch, seq, norm_mix, w_in, s5_A_re, s5_A_im, s5_log_dt, s5_B_re, s5_B_im, s5_C_re,
            s5_C_im, s5_D, s5_glu_w, s5_glu_b, ssd_conv_w, ssd_conv_b, ssd_dt_bias, ssd_A_log,
            ssd_D, ssd_norm_w, w_out, norm_ffn, w1, w3, w2):
    pieces = [(0, S5_WIDTH), (S5_WIDTH, SSD_WIDTH), (S5_WIDTH + SSD_WIDTH, SSD_CONV_CH),
              (S5_WIDTH + SSD_WIDTH + SSD_CONV_CH, LANES)]
    w_in_p = _pad_cols(w_in, pieces[-1][0] + LANES).astype(BF16)
    u, z, xbc, dt_raw = _norm_proj(h2d, norm_mix, w_in_p, pieces, 0, batch, seq)

    ab_re, ab_im, bc_re, bc_im = _s5_discretize(s5_A_re, s5_A_im, s5_log_dt, s5_B_re, s5_B_im)
    wb_re = _block_diag_halves(bc_re).astype(BF16)
    wb_im = _block_diag_halves(bc_im).astype(BF16)
    wc_re = _block_diag_halves(jnp.transpose(s5_C_re, (0, 2, 1))).astype(BF16)
    wc_im = _block_diag_halves(jnp.transpose(s5_C_im, (0, 2, 1))).astype(BF16)
    ya = _s5_mixer(u.reshape(seq * batch, S5_WIDTH), batch, seq, wb_re, wb_im,
                   ab_re.reshape(1, -1), ab_im.reshape(1, -1), wc_re, wc_im,
                   s5_D.reshape(1, S5_WIDTH), s5_glu_w.astype(BF16), s5_glu_b.reshape(1, S5_WIDTH))
    ya = ya.reshape(seq, batch * S5_WIDTH)

    yb = _ssd_mixer(z, xbc, dt_raw, batch, seq, ssd_conv_w, ssd_conv_b, ssd_dt_bias, ssd_A_log,
                    ssd_D, ssd_norm_w)
    h2d = _out_proj([ya, yb], [True, False], h2d, w_out.astype(BF16), batch, seq)
    return _ffn(h2d, norm_ffn, w1.astype(BF16), w3.astype(BF16), w2.astype(BF16))


def _layer1(h2d, batch, seq, norm_mix, w_in, swa_q_norm, swa_k_norm, swa_sinks, gla_gate_w2,
            gla_gate_b, gla_norm_w, w_out, norm_ffn, w1, w3, w2):
    widths = (SWA_WIDTH, SWA_KV_WIDTH, SWA_KV_WIDTH, GLA_KEY_WIDTH, GLA_KEY_WIDTH,
              GLA_VALUE_WIDTH, GLA_VALUE_WIDTH, LANES)
    starts = np.concatenate([[0], np.cumsum(widths)[:-1]]).tolist()
    pieces = list(zip(starts, widths))
    w_in_p = _pad_cols(w_in, starts[-1] + LANES).astype(BF16)
    qc, kc, vc, qd, kd, vd, gout, g_lr = _norm_proj(h2d, norm_mix, w_in_p, pieces, None, batch, seq)
    yc = _swa_mixer(qc, kc, vc, batch, seq, swa_q_norm, swa_k_norm, swa_sinks)
    yd = _gla_mixer(qd, kd, vd, gout, g_lr, batch, seq, gla_gate_w2, gla_gate_b, gla_norm_w)
    h2d = _out_proj([yc, yd], [False, False], h2d, w_out.astype(BF16), batch, seq)
    return _ffn(h2d, norm_ffn, w1.astype(BF16), w3.astype(BF16), w2.astype(BF16))


def kernel(x, norm0_mix, w_in0, s5_A_re, s5_A_im, s5_log_dt, s5_B_re, s5_B_im, s5_C_re, s5_C_im, s5_D, s5_glu_w, s5_glu_b, ssd_conv_w, ssd_conv_b, ssd_dt_bias, ssd_A_log, ssd_D, ssd_norm_w, w_out0, norm0_ffn, ffn0_w1, ffn0_w3, ffn0_w2, norm1_mix, w_in1, swa_q_norm, swa_k_norm, swa_sinks, gla_gate_w2, gla_gate_b, gla_norm_w, w_out1, norm1_ffn, ffn1_w1, ffn1_w3, ffn1_w2):
    batch, seq, d = x.shape
    assert batch == SUBLANES and d == D_MODEL and seq % OUT_ROW_BLOCK == 0
    h2d = x.reshape(batch * seq, d)
    h2d = _layer0(h2d, batch, seq, norm0_mix, w_in0, s5_A_re, s5_A_im, s5_log_dt, s5_B_re, s5_B_im,
                  s5_C_re, s5_C_im, s5_D, s5_glu_w, s5_glu_b, ssd_conv_w, ssd_conv_b, ssd_dt_bias,
                  ssd_A_log, ssd_D, ssd_norm_w, w_out0, norm0_ffn, ffn0_w1, ffn0_w3, ffn0_w2)
    h2d = _layer1(h2d, batch, seq, norm1_mix, w_in1, swa_q_norm, swa_k_norm, swa_sinks, gla_gate_w2,
                  gla_gate_b, gla_norm_w, w_out1, norm1_ffn, ffn1_w1, ffn1_w3, ffn1_w2)
    return h2d.reshape(batch, seq, d)
```

```python
import functools

import numpy as np
import jax
import jax.numpy as jnp
from jax import lax
from jax.experimental import pallas as pl
from jax.experimental.pallas import tpu as pltpu

F32 = jnp.float32
BF16 = jnp.bfloat16

LANES = 128
SUBLANES = 8
VMEM_LIMIT_BYTES = 56 * 1024 * 1024

D_MODEL = 2048
RMS_EPS = 1e-6
MIX_WIDTH = D_MODEL

S5_WIDTH = D_MODEL // 4
S5_GROUP_CH = 16
S5_GROUPS = S5_WIDTH // S5_GROUP_CH
S5_STATE = 64
S5_HALF_CH = S5_WIDTH // 2
S5_HALF_GROUPS = S5_GROUPS // 2
S5_HALF_STATES = S5_HALF_GROUPS * S5_STATE
S5_TIME_BLOCK = 128

SSD_WIDTH = MIX_WIDTH - S5_WIDTH
SSD_HEADDIM = 64
SSD_HEADS = SSD_WIDTH // SSD_HEADDIM
SSD_GROUPS = 4
SSD_HPG = SSD_HEADS // SSD_GROUPS
SSD_STATE = 128
SSD_CONV = 4
SSD_CHUNK = 128
SSD_BC_WIDTH = SSD_GROUPS * SSD_STATE
SSD_CONV_CH = SSD_WIDTH + 2 * SSD_BC_WIDTH
SSD_GROUP_WIDTH = SSD_WIDTH // SSD_GROUPS

SWA_HEADDIM = 64
SWA_WIDTH = MIX_WIDTH // 2
SWA_HEADS = SWA_WIDTH // SWA_HEADDIM
SWA_KV_HEADS = 4
SWA_HPG = SWA_HEADS // SWA_KV_HEADS
SWA_WINDOW = 128
SWA_KV_WIDTH = SWA_KV_HEADS * SWA_HEADDIM

GLA_VALUE_WIDTH = MIX_WIDTH - SWA_WIDTH
GLA_KEY_WIDTH = GLA_VALUE_WIDTH // 2
GLA_HEADS = 4
GLA_HEAD_K = GLA_KEY_WIDTH // GLA_HEADS
GLA_HEAD_V = GLA_VALUE_WIDTH // GLA_HEADS
GLA_GATE_RANK = 16
GLA_GATE_NORMALIZER = 16.0
GLA_CHUNK = 64
GLA_TIME_BLOCK = 256

FFN_HIDDEN = -(-(8 * D_MODEL) // (3 * 256)) * 256

PROJ_ROW_BLOCK = 256
FFN_ROW_BLOCK = 512
FFN_HIDDEN_BLOCK = 512
OUT_ROW_BLOCK = 512
PROJ_COL_CHUNK = 512

NEG_BIG = -1e30


def _params(*semantics):
    return pltpu.CompilerParams(dimension_semantics=semantics, vmem_limit_bytes=VMEM_LIMIT_BYTES)


def _vmem_spec():
    return pl.BlockSpec(memory_space=pltpu.VMEM)


def _dot(a, b):
    return jnp.dot(a, b, preferred_element_type=F32)


def _dot_nt(a, b):
    return lax.dot_general(a, b, (((1,), (1,)), ((), ())), preferred_element_type=F32)


def _dot_tn(a, b):
    return lax.dot_general(a, b, (((0,), (0,)), ((), ())), preferred_element_type=F32)


def _split3(v):
    hi = v.astype(BF16)
    r1 = v - hi.astype(F32)
    mid = r1.astype(BF16)
    lo = (r1 - mid.astype(F32)).astype(BF16)
    return hi, mid, lo


def _dot_exact_rhs01(v, m01):
    hi, mid, lo = _split3(v)
    return _dot(hi, m01) + _dot(mid, m01) + _dot(lo, m01)


def _dot_exact_lhs01(m01, v):
    hi, mid, lo = _split3(v)
    return _dot(m01, hi) + _dot(m01, mid) + _dot(m01, lo)


LOG2E = float(np.log2(np.e))


def _sigmoid(x):
    return 1.0 / (1.0 + jnp.exp2(x * (-LOG2E)))


def _silu(x):
    return x * _sigmoid(x)


def _softplus(x):
    return jnp.maximum(x, 0.0) + jnp.log1p(jnp.exp(-jnp.abs(x)))


def _gelu_tanh(x):
    c = np.sqrt(2.0 / np.pi).astype(np.float32)
    return 0.5 * x * (1.0 + jnp.tanh(c * (x + 0.044715 * (x * x * x))))


def _rmsnorm_rows(x, w):
    ms = jnp.mean(x * x, axis=-1, keepdims=True)
    return x * lax.rsqrt(ms + RMS_EPS) * w


PLAIN, SILU, CONV_SILU = "plain", "silu", "conv_silu"
def _norm_proj_kernel(pieces, blocks_per_seq, x_ref, g_ref, w_ref, convw_ref, convb_ref, *rest):
    n_out = sum(len(p[3]) for p in pieces)
    out_refs, xn_ref, tail_ref = rest[:n_out], rest[n_out], rest[n_out + 1]
    tm = x_ref.shape[0]
    xn_ref[...] = _rmsnorm_rows(x_ref[...], g_ref[...]).astype(BF16)
    @pl.when(pl.program_id(0) == 0)
    def _():
        tail_ref[...] = jnp.zeros_like(tail_ref)

    keep = jnp.where(pl.program_id(0) % blocks_per_seq == 0, 0.0, 1.0)

    def finish(val, kind, c0, cw, o_ref, off):
        if kind == CONV_SILU:
            ccols = slice(c0, c0 + cw)
            ext = jnp.concatenate([tail_ref[:, ccols] * keep, val], axis=0)
            tail_ref[:, ccols] = val[tm - SUBLANES:tm]
            val = convb_ref[:, ccols] + convw_ref[SSD_CONV - 1:SSD_CONV, ccols] * val
            for k in range(SSD_CONV - 1):
                back = SSD_CONV - 1 - k
                val = val + convw_ref[k:k + 1, ccols] * pltpu.roll(ext, back, 0)[SUBLANES:]
            val = _silu(val)
        elif kind == SILU:
            val = _silu(val)
        o_ref[:, c0 - off:c0 - off + cw] = val.astype(o_ref.dtype)

    pending = None
    out_i = 0
    for start, width, kind, splits in pieces:
        refs = out_refs[out_i:out_i + len(splits)]
        out_i += len(splits)
        for c0 in range(0, width, PROJ_COL_CHUNK):
            cw = min(PROJ_COL_CHUNK, width - c0)
            val = _dot(xn_ref[...], w_ref[:, start + c0:start + c0 + cw])
            if pending is not None:
                finish(*pending)
            (o_ref, off), = [(r, o) for r, (o, wd) in zip(refs, splits) if o <= c0 < o + wd]
            pending = (val, kind, c0, cw, o_ref, off)
    finish(*pending)


def _norm_proj(x2d, g, w_bf16, pieces, batch, seq, conv_w=None, conv_b=None):
    m, d = x2d.shape
    tm = PROJ_ROW_BLOCK
    n_l = seq // tm
    if conv_w is None:
        assert not any(p[2] == CONV_SILU for p in pieces)
        conv_w = jnp.zeros((SSD_CONV, LANES), F32)
        conv_b = jnp.zeros((1, LANES), F32)
    out_shapes, out_specs, kernel_pieces = [], [], []
    for start, width, kind, outs in pieces:
        kernel_pieces.append((start, width, kind, tuple((o, wd) for o, wd, _, _ in outs)))
        for off, wd, dtype, time_major in outs:
            assert off % PROJ_COL_CHUNK == 0 or len(outs) == 1
            if time_major:
                out_shapes.append(jax.ShapeDtypeStruct((seq, batch * wd), dtype))
                out_specs.append(pl.BlockSpec((tm, wd), lambda i: (i % n_l, i // n_l)))
            else:
                out_shapes.append(jax.ShapeDtypeStruct((m, wd), dtype))
                out_specs.append(pl.BlockSpec((tm, wd), lambda i: (i, 0)))
    return pl.pallas_call(
        functools.partial(_norm_proj_kernel, tuple(kernel_pieces), n_l),
        out_shape=tuple(out_shapes),
        grid=(m // tm,),
        in_specs=[pl.BlockSpec((tm, d), lambda i: (i, 0)),
                  pl.BlockSpec((1, d), lambda i: (0, 0)),
                  _vmem_spec(), _vmem_spec(), _vmem_spec()],
        out_specs=tuple(out_specs),
        scratch_shapes=[pltpu.VMEM((tm, d), BF16), pltpu.VMEM((SUBLANES, conv_w.shape[1]), F32)],
        compiler_params=_params("arbitrary"),
        name="norm_proj",
    )(x2d, g.reshape(1, d), w_bf16, conv_w, conv_b)


def _out_proj_kernel(splits, *refs):
    n = len(splits)
    y_refs, h_ref, w_ref, g_ref, o_ref, xn_ref = refs[:n], refs[n], refs[n + 1], refs[n + 2], refs[n + 3], refs[n + 4]
    acc = h_ref[...]
    for (start, width), y_ref in zip(splits, y_refs):
        acc = acc + _dot(y_ref[...].astype(BF16), w_ref[start:start + width, :])
    o_ref[...] = acc
    xn_ref[...] = _rmsnorm_rows(acc, g_ref[...]).astype(BF16)


def _out_proj(ys, time_major, h2d, w_bf16, g_next, batch, seq):
    m, d = h2d.shape
    tm = OUT_ROW_BLOCK
    n_l = seq // tm
    splits, in_specs, start = [], [], 0
    for y, tmaj in zip(ys, time_major):
        width = y.shape[1] // batch if tmaj else y.shape[1]
        splits.append((start, width))
        start += width
        if tmaj:
            in_specs.append(pl.BlockSpec((tm, width), lambda i: (i % n_l, i // n_l)))
        else:
            in_specs.append(pl.BlockSpec((tm, width), lambda i: (i, 0)))
    in_specs += [pl.BlockSpec((tm, d), lambda i: (i, 0)), _vmem_spec(), pl.BlockSpec((1, d), lambda i: (0, 0))]
    row_spec = pl.BlockSpec((tm, d), lambda i: (i, 0))
    return pl.pallas_call(
        functools.partial(_out_proj_kernel, tuple(splits)),
        out_shape=(jax.ShapeDtypeStruct((m, d), F32), jax.ShapeDtypeStruct((m, d), BF16)),
        grid=(m // tm,),
        in_specs=in_specs,
        out_specs=(row_spec, row_spec),
        compiler_params=_params("parallel"),
        name="out_proj",
    )(*ys, h2d, w_bf16, g_next.reshape(1, d))


def _ffn_kernel(h_ref, xn_ref, w1_ref, w3_ref, w2_ref, o_ref):
    @pl.when(pl.program_id(1) == 0)
    def _():
        o_ref[...] = h_ref[...]

    xn = xn_ref[...]
    a = _dot(xn, w1_ref[...])
    b = _dot(xn, w3_ref[...])
    act = (_silu(a) * b).astype(BF16)
    o_ref[...] += _dot(act, w2_ref[...])


def _ffn(h2d, xn, w1, w3, w2):
    m, d = h2d.shape
    hidden = w1.shape[1]
    tm, th = FFN_ROW_BLOCK, FFN_HIDDEN_BLOCK
    return pl.pallas_call(
        _ffn_kernel,
        out_shape=jax.ShapeDtypeStruct((m, d), F32),
        grid=(m // tm, hidden // th),
        in_specs=[pl.BlockSpec((tm, d), lambda i, j: (i, 0)),
                  pl.BlockSpec((tm, d), lambda i, j: (i, 0)),
                  pl.BlockSpec((d, th), lambda i, j: (0, j)),
                  pl.BlockSpec((d, th), lambda i, j: (0, j)),
                  pl.BlockSpec((th, d), lambda i, j: (j, 0))],
        out_specs=pl.BlockSpec((tm, d), lambda i, j: (i, 0)),
        compiler_params=_params("parallel", "arbitrary"),
        name="ffn",
    )(h2d, xn, w1, w3, w2)


def _s5_discretize_kernel(are_ref, aim_ref, logdt_ref, bre_ref, bim_ref,
                          abre_ref, abim_ref, bcre_ref, bcim_ref):
    a_re, a_im = are_ref[...], aim_ref[...]
    dt = jnp.exp(logdt_ref[...])
    mag = jnp.exp(a_re * dt)
    ab_re = mag * jnp.cos(a_im * dt)
    ab_im = mag * jnp.sin(a_im * dt)
    e_re, e_im = ab_re - 1.0, ab_im
    den = a_re * a_re + a_im * a_im
    coef_re = (e_re * a_re + e_im * a_im) / den
    coef_im = (e_im * a_re - e_re * a_im) / den
    abre_ref[...] = ab_re
    abim_ref[...] = ab_im
    b_re, b_im = bre_ref[...], bim_ref[...]
    c_re, c_im = coef_re[:, None, :], coef_im[:, None, :]
    bcre_ref[...] = c_re * b_re - c_im * b_im
    bcim_ref[...] = c_re * b_im + c_im * b_re


def _s5_discretize(a_re, a_im, log_dt, b_re, b_im):
    g, n = a_re.shape
    c = b_re.shape[2]
    b_re_t = jnp.transpose(b_re, (0, 2, 1))
    b_im_t = jnp.transpose(b_im, (0, 2, 1))
    return pl.pallas_call(
        _s5_discretize_kernel,
        out_shape=(jax.ShapeDtypeStruct((g, n), F32), jax.ShapeDtypeStruct((g, n), F32),
                   jax.ShapeDtypeStruct((g, c, n), F32), jax.ShapeDtypeStruct((g, c, n), F32)),
        name="s5_discretize",
    )(a_re, a_im, log_dt.reshape(g, 1), b_re_t, b_im_t)


def _block_diag_halves(blocks):
    g, r, c = blocks.shape
    hg = g // 2
    eye = jnp.eye(hg, dtype=blocks.dtype)
    b = blocks.reshape(2, hg, r, c)
    return jnp.einsum('hgrc,gk->hgrkc', b, eye).reshape(2, hg * r, hg * c)


def _s5_kernel(u_ref, wbre_ref, wbim_ref, abre_ref, abim_ref, wcre_ref, wcim_ref,
               d_ref, gw_ref, gb_ref, o_ref, xre_ref, xim_ref, hre_ref, him_ref):
    @pl.when(pl.program_id(0) == 0)
    def _():
        hre_ref[...] = jnp.zeros_like(hre_ref)
        him_ref[...] = jnp.zeros_like(him_ref)

    rows = u_ref.shape[0]
    n_steps = rows // SUBLANES
    u = u_ref[...]
    u_bf = u.astype(BF16)
    hs = S5_HALF_STATES
    y_halves = []
    for half in range(2):
        uh = u_bf[:, half * S5_HALF_CH:(half + 1) * S5_HALF_CH]
        xre_ref[...] = _dot(uh, wbre_ref[half])
        xim_ref[...] = _dot(uh, wbim_ref[half])
        a_re = jnp.broadcast_to(abre_ref[:, half * hs:(half + 1) * hs], (SUBLANES, hs))
        a_im = jnp.broadcast_to(abim_ref[:, half * hs:(half + 1) * hs], (SUBLANES, hs))

        def step(t, carry):
            h_re, h_im = carry
            r0 = pl.multiple_of(t * SUBLANES, SUBLANES)
            n_re = a_re * h_re - a_im * h_im + xre_ref[pl.ds(r0, SUBLANES), :]
            n_im = a_re * h_im + a_im * h_re + xim_ref[pl.ds(r0, SUBLANES), :]
            xre_ref[pl.ds(r0, SUBLANES), :] = n_re
            xim_ref[pl.ds(r0, SUBLANES), :] = n_im
            return n_re, n_im

        h0 = (hre_ref[:, half * hs:(half + 1) * hs], him_ref[:, half * hs:(half + 1) * hs])
        h_re, h_im = lax.fori_loop(0, n_steps, step, h0, unroll=4)
        hre_ref[:, half * hs:(half + 1) * hs] = h_re
        him_ref[:, half * hs:(half + 1) * hs] = h_im
        y_halves.append(_dot(xre_ref[...].astype(BF16), wcre_ref[half])
                        - _dot(xim_ref[...].astype(BF16), wcim_ref[half]))
    y = jnp.concatenate(y_halves, axis=-1) + d_ref[...] * u
    y = _gelu_tanh(y)
    gate = _sigmoid(_dot(y.astype(BF16), gw_ref[...]) + gb_ref[...])
    o_ref[...] = y * gate


def _s5_mixer(u_tb, batch, seq, wb_re, wb_im, ab_re, ab_im, wc_re, wc_im, d_skip, glu_w, glu_b):
    rows = S5_TIME_BLOCK * batch
    width = S5_WIDTH
    return pl.pallas_call(
        _s5_kernel,
        out_shape=jax.ShapeDtypeStruct((seq * batch, width), F32),
        grid=(seq // S5_TIME_BLOCK,),
        in_specs=[pl.BlockSpec((rows, width), lambda i: (i, 0))] + [_vmem_spec()] * 9,
        out_specs=pl.BlockSpec((rows, width), lambda i: (i, 0)),
        scratch_shapes=[pltpu.VMEM((rows, S5_HALF_STATES), F32),
                        pltpu.VMEM((rows, S5_HALF_STATES), F32),
                        pltpu.VMEM((batch, 2 * S5_HALF_STATES), F32),
                        pltpu.VMEM((batch, 2 * S5_HALF_STATES), F32)],
        compiler_params=_params("arbitrary"),
        name="s5_mixer",
    )(u_tb, wb_re, wb_im, ab_re, ab_im, wc_re, wc_im, d_skip, glu_w, glu_b)


def _ssd_kernel(zs_ref, xs_ref, b_ref, c_ref, dtraw_ref, dtb_ref, alog_ref, dexp_ref,
                normw_ref, tril_ref, expand_ref, o_ref, state_ref):
    q = SSD_CHUNK
    p = SSD_HEADDIM

    @pl.when(pl.program_id(1) == 0)
    def _():
        state_ref[...] = jnp.zeros_like(state_ref)

    xs = xs_ref[...]
    b_bf = b_ref[...]
    c_bf = c_ref[...]

    dt = _softplus(dtraw_ref[...] + dtb_ref[...])
    dt_a = dt * (-jnp.exp(alog_ref[...]) * LOG2E)
    hp = dt.shape[1]
    cs3 = _dot(tril_ref[...], jnp.concatenate(_split3(dt_a), axis=1))
    cs = cs3[:, :hp] + cs3[:, hp:2 * hp] + cs3[:, 2 * hp:]
    cs_t = cs.T
    cs_e = _dot(jnp.concatenate(_split3(cs), axis=1), expand_ref[...])
    dt_hi = dt.astype(BF16)
    dt_lo = (dt - dt_hi.astype(F32)).astype(BF16)
    dt_e = _dot(jnp.concatenate([dt_hi, dt_lo], axis=1), expand_ref[0:2 * hp, :])
    xdt = xs * dt_e
    xdt_bf = xdt.astype(BF16)
    cs_last_e = cs_e[q - 1:q, :]
    decay_in = jnp.exp2(cs_e)
    chunk_decay = jnp.exp2(cs_last_e)
    x_end = (xdt * jnp.exp2(cs_last_e - cs_e)).astype(BF16)

    row_i = lax.broadcasted_iota(jnp.int32, (q, q), 0)
    col_i = lax.broadcasted_iota(jnp.int32, (q, q), 1)
    causal = row_i >= col_i
    first_of_pair = lax.broadcasted_iota(jnp.int32, (q, 2 * p), 1) < p

    y_groups = []
    for g in range(SSD_GROUPS):
        bg = b_bf[:, g * SSD_STATE:(g + 1) * SSD_STATE]
        cg = c_bf[:, g * SSD_STATE:(g + 1) * SSD_STATE]
        gsl = slice(g * SSD_GROUP_WIDTH, (g + 1) * SSD_GROUP_WIDTH)
        cb = _dot_nt(cg, bg)
        state = state_ref[g]
        y_off = _dot(cg, state.astype(BF16)) * decay_in[:, gsl]
        y_pairs = []
        for pr in range(SSD_HPG // 2):
            h0 = g * SSD_HPG + 2 * pr
            xp = xdt_bf[:, h0 * p:(h0 + 2) * p]
            parts = []
            for h in (h0, h0 + 1):
                seg = jnp.broadcast_to(cs[:, h:h + 1], (q, q)) - jnp.broadcast_to(cs_t[h:h + 1, :], (q, q))
                decay = jnp.exp2(jnp.where(causal, seg, NEG_BIG))
                parts.append(_dot((cb * decay).astype(BF16), xp))
            y_pairs.append(jnp.where(first_of_pair, parts[0], parts[1]))
        y_groups.append(jnp.concatenate(y_pairs, axis=-1) + y_off)
        new_state = _dot_tn(bg, x_end[:, gsl])
        state_ref[g] = state * chunk_decay[:, gsl] + new_state

    y = jnp.concatenate(y_groups, axis=-1) + dexp_ref[...] * xs
    y = y * zs_ref[...]
    outs = []
    for g in range(SSD_GROUPS):
        gsl = slice(g * SSD_GROUP_WIDTH, (g + 1) * SSD_GROUP_WIDTH)
        yg = y[:, gsl]
        ms = jnp.mean(yg * yg, axis=-1, keepdims=True)
        outs.append(yg * lax.rsqrt(ms + RMS_EPS) * normw_ref[:, gsl])
    o_ref[...] = jnp.concatenate(outs, axis=-1)


def _ssd_mixer(zs, xs, b_act, c_act, dt_raw, batch, seq, dt_bias, a_log, d_skip, norm_w):
    q = SSD_CHUNK
    n_c = seq // q
    heads_pad = LANES
    pad_h = heads_pad - SSD_HEADS
    dtb = jnp.pad(dt_bias, (0, pad_h)).reshape(1, heads_pad)
    alog = jnp.pad(a_log, (0, pad_h)).reshape(1, heads_pad)
    d_exp = jnp.repeat(d_skip, SSD_HEADDIM).reshape(1, SSD_WIDTH)
    t = jnp.arange(q)
    tril = (t[:, None] >= t[None, :]).astype(BF16)
    expand = (jnp.arange(heads_pad)[:, None] == (jnp.arange(SSD_WIDTH) // SSD_HEADDIM)[None, :]).astype(BF16)
    expand3 = jnp.tile(expand, (3, 1))
    row_map = lambda b, c: (b * n_c + c, 0)
    return pl.pallas_call(
        _ssd_kernel,
        out_shape=jax.ShapeDtypeStruct((batch * seq, SSD_WIDTH), F32),
        grid=(batch, n_c),
        in_specs=[pl.BlockSpec((q, SSD_WIDTH), row_map),
                  pl.BlockSpec((q, SSD_WIDTH), row_map),
                  pl.BlockSpec((q, SSD_BC_WIDTH), row_map),
                  pl.BlockSpec((q, SSD_BC_WIDTH), row_map),
                  pl.BlockSpec((q, heads_pad), row_map)] + [_vmem_spec()] * 6,
        out_specs=pl.BlockSpec((q, SSD_WIDTH), row_map),
        scratch_shapes=[pltpu.VMEM((SSD_GROUPS, SSD_STATE, SSD_GROUP_WIDTH), F32)],
        compiler_params=_params("parallel", "arbitrary"),
        name="ssd_mixer",
    )(zs, xs, b_act, c_act, dt_raw, dtb, alog, d_exp, norm_w.reshape(1, SSD_WIDTH), tril, expand3)


def _head_rmsnorm(x, w_tiled, seg_ones, head_dim):
    sq = x * x
    hi = sq.astype(BF16)
    lo = (sq - hi.astype(F32)).astype(BF16)
    width = x.shape[1]
    blk = seg_ones.shape[0]
    parts = []
    for c0 in range(0, width, blk):
        parts.append(_dot(hi[:, c0:c0 + blk], seg_ones) + _dot(lo[:, c0:c0 + blk], seg_ones))
    ss = parts[0] if len(parts) == 1 else jnp.concatenate(parts, axis=-1)
    return x * lax.rsqrt(ss * (1.0 / head_dim) + RMS_EPS) * w_tiled


def _swa_kernel(q_ref, k_ref, v_ref, kprev_ref, vprev_ref, qw_ref, kw_ref, sink_ref, seg_ref, rep_ref, o_ref):
    blk = SWA_WINDOW
    hd = SWA_HEADDIM
    slab = SWA_HPG * hd
    first = pl.program_id(1) == 0

    seg_ones = seg_ref[...]
    qn = _head_rmsnorm(q_ref[...], qw_ref[...], seg_ones, hd) * (hd ** -0.5)
    k_both = jnp.concatenate([kprev_ref[...], k_ref[...]], axis=0)
    k_cat = _head_rmsnorm(k_both, kw_ref[...], seg_ones, hd).astype(BF16)
    v_cat = jnp.concatenate([vprev_ref[...], v_ref[...]], axis=0).astype(BF16)

    qpos = lax.broadcasted_iota(jnp.int32, (blk, blk), 0)
    kpos = lax.broadcasted_iota(jnp.int32, (blk, blk), 1)
    in_prev = kpos > qpos
    prev_bias = jnp.where(first, NEG_BIG, 0.0)
    lane = lax.broadcasted_iota(jnp.int32, (blk, slab), 1)
    head_masks = [(lane >= r * hd) & (lane < (r + 1) * hd) for r in range(SWA_HPG)]

    scores, v_reps = [], []
    for g in range(SWA_KV_HEADS):
        rep = rep_ref[g]
        k_rep = _dot(k_cat, rep).astype(BF16)
        v_reps.append(_dot(v_cat, rep).astype(BF16))
        q_slab = qn[:, g * slab:(g + 1) * slab]
        q_stack = jnp.concatenate([jnp.where(mk, q_slab, 0.0) for mk in head_masks], axis=0).astype(BF16)
        scores.append(_dot_nt(q_stack, k_rep))

    probs = []
    for g in range(SWA_KV_HEADS):
        p_heads = []
        for r in range(SWA_HPG):
            s = scores[g][r * blk:(r + 1) * blk, :]
            s_fold = jnp.where(in_prev, s[:, :blk] + prev_bias, s[:, blk:])
            sink = sink_ref[g * SWA_HPG + r]
            m = jnp.maximum(jnp.max(s_fold, axis=-1, keepdims=True), sink)
            p = jnp.exp(s_fold - m)
            denom = jnp.sum(p, axis=-1, keepdims=True) + jnp.exp(sink - m)
            pn = p * (1.0 / denom)
            p_heads.append(jnp.concatenate([jnp.where(in_prev, pn, 0.0).astype(BF16),
                                            jnp.where(in_prev, 0.0, pn).astype(BF16)], axis=1))
        probs.append(jnp.concatenate(p_heads, axis=0))

    for g in range(SWA_KV_HEADS):
        o_stack = _dot(probs[g], v_reps[g])
        o_slab = jnp.where(head_masks[0], o_stack[0:blk], 0.0)
        for r in range(1, SWA_HPG):
            o_slab = o_slab + jnp.where(head_masks[r], o_stack[r * blk:(r + 1) * blk], 0.0)
        o_ref[:, g * slab:(g + 1) * slab] = o_slab


def _swa_mixer(q, k, v, batch, seq, q_norm_w, k_norm_w, sinks):
    blk = SWA_WINDOW
    n_b = seq // blk
    qw = jnp.tile(q_norm_w, SWA_HEADS).reshape(1, SWA_WIDTH)
    kw = jnp.tile(k_norm_w, SWA_KV_HEADS).reshape(1, SWA_KV_WIDTH)
    seg = SWA_KV_WIDTH
    lane = jnp.arange(seg)
    seg_ones = (lane[:, None] // SWA_HEADDIM == lane[None, :] // SWA_HEADDIM).astype(BF16)
    rep = (lane[None, :, None] == (jnp.arange(SWA_KV_HEADS)[:, None, None] * SWA_HEADDIM
                                   + lane[None, None, :] % SWA_HEADDIM)).astype(BF16)
    row_map = lambda b, i: (b * n_b + i, 0)
    prev_map = lambda b, i: (b * n_b + jnp.maximum(i - 1, 0), 0)
    return pl.pallas_call(
        _swa_kernel,
        out_shape=jax.ShapeDtypeStruct((batch * seq, SWA_WIDTH), F32),
        grid=(batch, n_b),
        in_specs=[pl.BlockSpec((blk, SWA_WIDTH), row_map),
                  pl.BlockSpec((blk, SWA_KV_WIDTH), row_map),
                  pl.BlockSpec((blk, SWA_KV_WIDTH), row_map),
                  pl.BlockSpec((blk, SWA_KV_WIDTH), prev_map),
                  pl.BlockSpec((blk, SWA_KV_WIDTH), prev_map),
                  _vmem_spec(), _vmem_spec(),
                  pl.BlockSpec(memory_space=pltpu.SMEM),
                  _vmem_spec(), _vmem_spec()],
        out_specs=pl.BlockSpec((blk, SWA_WIDTH), row_map),
        compiler_params=_params("parallel", "parallel"),
        name="swa_mixer",
    )(q, k, v, k, v, qw, kw, sinks, seg_ones, rep)


def _gla_kernel(q_ref, k_ref, v_ref, gate_ref, glr_ref, w2_ref, gb_ref, normw_ref, tril_ref,
                o_ref, state_ref):
    c = GLA_CHUNK
    dk, dv = GLA_HEAD_K, GLA_HEAD_V
    rows = q_ref.shape[0]

    @pl.when(pl.program_id(1) == 0)
    def _():
        state_ref[...] = jnp.zeros_like(state_ref)

    w2 = w2_ref[...]
    g_hi, g_mid, g_lo = _split3(glr_ref[...])
    w_hi, w_mid, w_lo = _split3(w2)
    pre = (_dot(g_hi, w_hi) + _dot(g_hi, w_mid) + _dot(g_mid, w_hi)
           + _dot(g_hi, w_lo) + _dot(g_mid, w_mid) + _dot(g_lo, w_hi)) + gb_ref[...]
    gk = -_softplus(-pre) * (LOG2E / GLA_GATE_NORMALIZER)
    gk3 = _dot(tril_ref[...], jnp.concatenate(_split3(gk), axis=1))
    kw = gk.shape[1]
    bc = gk3[:, :kw] + gk3[:, kw:2 * kw] + gk3[:, 2 * kw:]

    n_chunks = rows // c
    pair = 2 * c
    assert rows % pair == 0
    b_lasts = [bc[(i + 1) * c - 1:(i + 1) * c, :] for i in range(n_chunks)]
    b_last = jnp.concatenate([jnp.broadcast_to(b, (c, kw)) for b in b_lasts], axis=0)
    k_all = k_ref[...]
    q_t = (q_ref[...] * (dk ** -0.5) * jnp.exp2(bc)).astype(BF16)
    k_t = (k_all * jnp.exp2(-bc)).astype(BF16)
    k_end = (k_all * jnp.exp2(b_last - bc)).astype(BF16)
    decs = [jnp.exp2(b) for b in b_lasts]
    v_bf = v_ref[...].astype(BF16)
    row_i = lax.broadcasted_iota(jnp.int32, (pair, pair), 0)
    col_i = lax.broadcasted_iota(jnp.int32, (pair, pair), 1)
    same_chunk_causal = (row_i >= col_i) & (col_i >= (row_i // c) * c)

    heads = []
    for h in range(GLA_HEADS):
        ksl = slice(h * dk, (h + 1) * dk)
        vsl = slice(h * dv, (h + 1) * dv)
        o_intra = []
        for r0 in range(0, rows, pair):
            rsl = slice(r0, r0 + pair)
            att = jnp.where(same_chunk_causal, _dot_nt(q_t[rsl, ksl], k_t[rsl, ksl]), 0.0).astype(BF16)
            o_intra.append(_dot(att, v_bf[rsl, vsl]))
        state = state_ref[h]
        o_inter = []
        for i in range(n_chunks):
            rsl = slice(i * c, (i + 1) * c)
            o_inter.append(_dot_nt(q_t[rsl, ksl], state.astype(BF16)))
            state = state * decs[i][:, ksl] + _dot_tn(v_bf[rsl, vsl], k_end[rsl, ksl])
        state_ref[h] = state
        o = jnp.concatenate(o_intra, axis=0) + jnp.concatenate(o_inter, axis=0)
        ms = jnp.mean(o * o, axis=-1, keepdims=True)
        heads.append(o * lax.rsqrt(ms + RMS_EPS) * normw_ref[...])
    o_ref[...] = jnp.concatenate(heads, axis=-1) * gate_ref[...]


def _gla_mixer(q, k, v, gate, g_lr, batch, seq, gate_w2, gate_b, norm_w):
    tq = GLA_TIME_BLOCK
    n_t = seq // tq
    rank_pad = LANES
    w2 = jnp.pad(gate_w2, ((0, rank_pad - GLA_GATE_RANK), (0, 0)))
    idx = jnp.arange(tq)
    tril = ((idx[:, None] >= idx[None, :]) & (idx[:, None] // GLA_CHUNK == idx[None, :] // GLA_CHUNK)).astype(BF16)
    row_map = lambda b, i: (b * n_t + i, 0)
    return pl.pallas_call(
        _gla_kernel,
        out_shape=jax.ShapeDtypeStruct((batch * seq, GLA_VALUE_WIDTH), F32),
        grid=(batch, n_t),
        in_specs=[pl.BlockSpec((tq, GLA_KEY_WIDTH), row_map),
                  pl.BlockSpec((tq, GLA_KEY_WIDTH), row_map),
                  pl.BlockSpec((tq, GLA_VALUE_WIDTH), row_map),
                  pl.BlockSpec((tq, GLA_VALUE_WIDTH), row_map),
                  pl.BlockSpec((tq, rank_pad), row_map)] + [_vmem_spec()] * 4,
        out_specs=pl.BlockSpec((tq, GLA_VALUE_WIDTH), row_map),
        scratch_shapes=[pltpu.VMEM((GLA_HEADS, GLA_HEAD_V, GLA_HEAD_K), F32)],
        compiler_params=_params("parallel", "arbitrary"),
        name="gla_mixer",
    )(q, k, v, gate, g_lr, w2, gate_b.reshape(1, GLA_KEY_WIDTH), norm_w.reshape(1, GLA_HEAD_V), tril)


def _pad_cols(w, total):
    return jnp.pad(w, ((0, 0), (0, total - w.shape[1])))


def _layer0(h2d, batch, seq, norm_mix, w_in, s5_A_re, s5_A_im, s5_log_dt, s5_B_re, s5_B_im, s5_C_re,
            s5_C_im, s5_D, s5_glu_w, s5_glu_b, ssd_conv_w, ssd_conv_b, ssd_dt_bias, ssd_A_log,
            ssd_D, ssd_norm_w, w_out, norm_ffn, w1, w3, w2):
    z0 = S5_WIDTH
    x0 = z0 + SSD_WIDTH
    dt0 = x0 + SSD_CONV_CH
    pieces = [
        (x0, SSD_CONV_CH, CONV_SILU, [(0, SSD_WIDTH, F32, False),
                                      (SSD_WIDTH, SSD_BC_WIDTH, BF16, False),
                                      (SSD_WIDTH + SSD_BC_WIDTH, SSD_BC_WIDTH, BF16, False)]),
        (0, S5_WIDTH, PLAIN, [(0, S5_WIDTH, F32, True)]),
        (z0, SSD_WIDTH, SILU, [(0, SSD_WIDTH, F32, False)]),
        (dt0, LANES, PLAIN, [(0, LANES, F32, False)]),
    ]
    w_in_p = _pad_cols(w_in, dt0 + LANES).astype(BF16)
    xs, b_act, c_act, u, zs, dt_raw = _norm_proj(h2d, norm_mix, w_in_p, pieces, batch, seq,
                                                 ssd_conv_w, ssd_conv_b.reshape(1, SSD_CONV_CH))

    ab_re, ab_im, bc_re, bc_im = _s5_discretize(s5_A_re, s5_A_im, s5_log_dt, s5_B_re, s5_B_im)
    wb_re = _block_diag_halves(bc_re).astype(BF16)
    wb_im = _block_diag_halves(bc_im).astype(BF16)
    wc_re = _block_diag_halves(jnp.transpose(s5_C_re, (0, 2, 1))).astype(BF16)
    wc_im = _block_diag_halves(jnp.transpose(s5_C_im, (0, 2, 1))).astype(BF16)
    ya = _s5_mixer(u.reshape(seq * batch, S5_WIDTH), batch, seq, wb_re, wb_im,
                   ab_re.reshape(1, -1), ab_im.reshape(1, -1), wc_re, wc_im,
                   s5_D.reshape(1, S5_WIDTH), s5_glu_w.astype(BF16), s5_glu_b.reshape(1, S5_WIDTH))
    ya = ya.reshape(seq, batch * S5_WIDTH)

    yb = _ssd_mixer(zs, xs, b_act, c_act, dt_raw, batch, seq, ssd_dt_bias, ssd_A_log, ssd_D, ssd_norm_w)
    h2d, xn = _out_proj([ya, yb], [True, False], h2d, w_out.astype(BF16), norm_ffn, batch, seq)
    return _ffn(h2d, xn, w1.astype(BF16), w3.astype(BF16), w2.astype(BF16))


def _layer1(h2d, batch, seq, norm_mix, w_in, swa_q_norm, swa_k_norm, swa_sinks, gla_gate_w2,
            gla_gate_b, gla_norm_w, w_out, norm_ffn, w1, w3, w2):
    widths = (SWA_WIDTH, SWA_KV_WIDTH, SWA_KV_WIDTH, GLA_KEY_WIDTH, GLA_KEY_WIDTH,
              GLA_VALUE_WIDTH, GLA_VALUE_WIDTH, LANES)
    starts = np.concatenate([[0], np.cumsum(widths)[:-1]]).tolist()
    kinds = (PLAIN,) * 6 + (SILU, PLAIN)
    pieces = [(s, w, kind, [(0, w, F32, False)]) for s, w, kind in zip(starts, widths, kinds)]
    w_in_p = _pad_cols(w_in, starts[-1] + LANES).astype(BF16)
    qc, kc, vc, qd, kd, vd, gate, g_lr = _norm_proj(h2d, norm_mix, w_in_p, pieces, batch, seq)
    yc = _swa_mixer(qc, kc, vc, batch, seq, swa_q_norm, swa_k_norm, swa_sinks)
    yd = _gla_mixer(qd, kd, vd, gate, g_lr, batch, seq, gla_gate_w2, gla_gate_b, gla_norm_w)
    h2d, xn = _out_proj([yc, yd], [False, False], h2d, w_out.astype(BF16), norm_ffn, batch, seq)
    return _ffn(h2d, xn, w1.astype(BF16), w3.astype(BF16), w2.astype(BF16))


def kernel(x, norm0_mix, w_in0, s5_A_re, s5_A_im, s5_log_dt, s5_B_re, s5_B_im, s5_C_re, s5_C_im, s5_D, s5_glu_w, s5_glu_b, ssd_conv_w, ssd_conv_b, ssd_dt_bias, ssd_A_log, ssd_D, ssd_norm_w, w_out0, norm0_ffn, ffn0_w1, ffn0_w3, ffn0_w2, norm1_mix, w_in1, swa_q_norm, swa_k_norm, swa_sinks, gla_gate_w2, gla_gate_b, gla_norm_w, w_out1, norm1_ffn, ffn1_w1, ffn1_w3, ffn1_w2):
    batch, seq, d = x.shape
    assert batch == SUBLANES and d == D_MODEL and seq % OUT_ROW_BLOCK == 0
    h2d = x.reshape(batch * seq, d)
    h2d = _layer0(h2d, batch, seq, norm0_mix, w_in0, s5_A_re, s5_A_im, s5_log_dt, s5_B_re, s5_B_im,
                  s5_C_re, s5_C_im, s5_D, s5_glu_w, s5_glu_b, ssd_conv_w, ssd_conv_b, ssd_dt_bias,
                  ssd_A_log, ssd_D, ssd_norm_w, w_out0, norm0_ffn, ffn0_w1, ffn0_w3, ffn0_w2)
    h2d = _layer1(h2d, batch, seq, norm1_mix, w_in1, swa_q_norm, swa_k_norm, swa_sinks, gla_gate_w2,
                  gla_gate_b, gla_norm_w, w_out1, norm1_ffn, ffn1_w1, ffn1_w3, ffn1_w2)
    return h2d.reshape(batch, seq, d)
```

```python
import functools

import numpy as np
import jax
import jax.numpy as jnp
from jax import lax
from jax.experimental import pallas as pl
from jax.experimental.pallas import tpu as pltpu

F32 = jnp.float32
BF16 = jnp.bfloat16

LANES = 128
SUBLANES = 8
VMEM_LIMIT_BYTES = 56 * 1024 * 1024

D_MODEL = 2048
RMS_EPS = 1e-6
MIX_WIDTH = D_MODEL

S5_WIDTH = D_MODEL // 4
S5_GROUP_CH = 16
S5_GROUPS = S5_WIDTH // S5_GROUP_CH
S5_STATE = 64
S5_HALF_CH = S5_WIDTH // 2
S5_HALF_GROUPS = S5_GROUPS // 2
S5_HALF_STATES = S5_HALF_GROUPS * S5_STATE
S5_TIME_BLOCK = 128

SSD_WIDTH = MIX_WIDTH - S5_WIDTH
SSD_HEADDIM = 64
SSD_HEADS = SSD_WIDTH // SSD_HEADDIM
SSD_GROUPS = 4
SSD_HPG = SSD_HEADS // SSD_GROUPS
SSD_STATE = 128
SSD_CONV = 4
SSD_CHUNK = 128
SSD_BC_WIDTH = SSD_GROUPS * SSD_STATE
SSD_CONV_CH = SSD_WIDTH + 2 * SSD_BC_WIDTH
SSD_GROUP_WIDTH = SSD_WIDTH // SSD_GROUPS

SWA_HEADDIM = 64
SWA_WIDTH = MIX_WIDTH // 2
SWA_HEADS = SWA_WIDTH // SWA_HEADDIM
SWA_KV_HEADS = 4
SWA_HPG = SWA_HEADS // SWA_KV_HEADS
SWA_WINDOW = 128
SWA_KV_WIDTH = SWA_KV_HEADS * SWA_HEADDIM
SWA_BLOCKS_PER_STEP = 4

GLA_VALUE_WIDTH = MIX_WIDTH - SWA_WIDTH
GLA_KEY_WIDTH = GLA_VALUE_WIDTH // 2
GLA_HEADS = 4
GLA_HEAD_K = GLA_KEY_WIDTH // GLA_HEADS
GLA_HEAD_V = GLA_VALUE_WIDTH // GLA_HEADS
GLA_GATE_RANK = 16
GLA_GATE_NORMALIZER = 16.0
GLA_CHUNK = 64
GLA_TIME_BLOCK = 256

FFN_HIDDEN = -(-(8 * D_MODEL) // (3 * 256)) * 256

PROJ_ROW_BLOCK = 256
FFN_ROW_BLOCK = 512
FFN_HIDDEN_BLOCK = 512
OUT_ROW_BLOCK = 512
PROJ_COL_CHUNK = 512

NEG_BIG = -1e30


def _params(*semantics):
    return pltpu.CompilerParams(dimension_semantics=semantics, vmem_limit_bytes=VMEM_LIMIT_BYTES)


def _vmem_spec():
    return pl.BlockSpec(memory_space=pltpu.VMEM)


def _dot(a, b):
    return jnp.dot(a, b, preferred_element_type=F32)


def _dot_nt(a, b):
    return lax.dot_general(a, b, (((1,), (1,)), ((), ())), preferred_element_type=F32)


def _dot_tn(a, b):
    return lax.dot_general(a, b, (((0,), (0,)), ((), ())), preferred_element_type=F32)


def _split3(v):
    hi = v.astype(BF16)
    r1 = v - hi.astype(F32)
    mid = r1.astype(BF16)
    lo = (r1 - mid.astype(F32)).astype(BF16)
    return hi, mid, lo


def _dot_exact_rhs01(v, m01):
    hi, mid, lo = _split3(v)
    return _dot(hi, m01) + _dot(mid, m01) + _dot(lo, m01)


def _dot_exact_lhs01(m01, v):
    hi, mid, lo = _split3(v)
    return _dot(m01, hi) + _dot(m01, mid) + _dot(m01, lo)


LOG2E = float(np.log2(np.e))


def _sigmoid(x):
    return 1.0 / (1.0 + jnp.exp2(x * (-LOG2E)))


def _silu(x):
    return x * _sigmoid(x)


def _softplus(x):
    return jnp.maximum(x, 0.0) + jnp.log1p(jnp.exp(-jnp.abs(x)))


def _gelu_tanh(x):
    c = np.sqrt(2.0 / np.pi).astype(np.float32)
    return 0.5 * x * (1.0 + jnp.tanh(c * (x + 0.044715 * (x * x * x))))


def _rmsnorm_rows(x, w):
    ms = jnp.mean(x * x, axis=-1, keepdims=True)
    return x * lax.rsqrt(ms + RMS_EPS) * w


PLAIN, SILU = "plain", "silu"


def _norm_proj_kernel(pieces, x_ref, g_ref, w_ref, *rest):
    out_refs = rest[:len(pieces)]
    xn_ref = rest[len(pieces)]
    xn_ref[...] = _rmsnorm_rows(x_ref[...], g_ref[...]).astype(BF16)
    for (start, width, kind), o_ref in zip(pieces, out_refs):
        for c0 in range(0, width, PROJ_COL_CHUNK):
            cw = min(PROJ_COL_CHUNK, width - c0)
            val = _dot(xn_ref[...], w_ref[:, start + c0:start + c0 + cw])
            o_ref[:, c0:c0 + cw] = _silu(val) if kind == SILU else val


def _norm_proj(x2d, g, w_bf16, pieces, time_major_piece, batch, seq):
    m, d = x2d.shape
    tm = PROJ_ROW_BLOCK
    n_l = seq // tm
    out_shapes, out_specs = [], []
    for idx, (_, width, _) in enumerate(pieces):
        if idx == time_major_piece:
            out_shapes.append(jax.ShapeDtypeStruct((seq, batch * width), F32))
            out_specs.append(pl.BlockSpec((tm, width), lambda i: (i % n_l, i // n_l)))
        else:
            out_shapes.append(jax.ShapeDtypeStruct((m, width), F32))
            out_specs.append(pl.BlockSpec((tm, width), lambda i: (i, 0)))
    return pl.pallas_call(
        functools.partial(_norm_proj_kernel, tuple(pieces)),
        out_shape=tuple(out_shapes),
        grid=(m // tm,),
        in_specs=[pl.BlockSpec((tm, d), lambda i: (i, 0)),
                  pl.BlockSpec((1, d), lambda i: (0, 0)),
                  _vmem_spec()],
        out_specs=tuple(out_specs),
        scratch_shapes=[pltpu.VMEM((tm, d), BF16)],
        compiler_params=_params("parallel"),
        name="norm_proj",
    )(x2d, g.reshape(1, d), w_bf16)


def _out_proj_kernel(splits, *refs):
    n = len(splits)
    y_refs, h_ref, w_ref, o_ref = refs[:n], refs[n], refs[n + 1], refs[n + 2]
    acc = h_ref[...]
    for (start, width), y_ref in zip(splits, y_refs):
        acc = acc + _dot(y_ref[...].astype(BF16), w_ref[start:start + width, :])
    o_ref[...] = acc


def _out_proj(ys, time_major, h2d, w_bf16, batch, seq):
    m, d = h2d.shape
    tm = OUT_ROW_BLOCK
    n_l = seq // tm
    splits, in_specs, start = [], [], 0
    for y, tmaj in zip(ys, time_major):
        width = y.shape[1] // batch if tmaj else y.shape[1]
        splits.append((start, width))
        start += width
        if tmaj:
            in_specs.append(pl.BlockSpec((tm, width), lambda i: (i % n_l, i // n_l)))
        else:
            in_specs.append(pl.BlockSpec((tm, width), lambda i: (i, 0)))
    in_specs += [pl.BlockSpec((tm, d), lambda i: (i, 0)), _vmem_spec()]
    return pl.pallas_call(
        functools.partial(_out_proj_kernel, tuple(splits)),
        out_shape=jax.ShapeDtypeStruct((m, d), F32),
        grid=(m // tm,),
        in_specs=in_specs,
        out_specs=pl.BlockSpec((tm, d), lambda i: (i, 0)),
        compiler_params=_params("parallel"),
        name="out_proj",
    )(*ys, h2d, w_bf16)


def _ffn_kernel(h_ref, g_ref, w1_ref, w3_ref, w2_ref, o_ref, xn_ref, acc_ref):
    j = pl.program_id(1)

    @pl.when(j == 0)
    def _():
        xn_ref[...] = _rmsnorm_rows(h_ref[...], g_ref[...]).astype(BF16)
        acc_ref[...] = jnp.zeros_like(acc_ref)

    xn = xn_ref[...]
    a = _dot(xn, w1_ref[...])
    b = _dot(xn, w3_ref[...])
    act = (_silu(a) * b).astype(BF16)
    acc_ref[...] += _dot(act, w2_ref[...])

    @pl.when(j == pl.num_programs(1) - 1)
    def _():
        o_ref[...] = h_ref[...] + acc_ref[...]


def _ffn(h2d, g, w1, w3, w2):
    m, d = h2d.shape
    hidden = w1.shape[1]
    tm, th = FFN_ROW_BLOCK, FFN_HIDDEN_BLOCK
    return pl.pallas_call(
        _ffn_kernel,
        out_shape=jax.ShapeDtypeStruct((m, d), F32),
        grid=(m // tm, hidden // th),
        in_specs=[pl.BlockSpec((tm, d), lambda i, j: (i, 0)),
                  pl.BlockSpec((1, d), lambda i, j: (0, 0)),
                  pl.BlockSpec((d, th), lambda i, j: (0, j)),
                  pl.BlockSpec((d, th), lambda i, j: (0, j)),
                  pl.BlockSpec((th, d), lambda i, j: (j, 0))],
        out_specs=pl.BlockSpec((tm, d), lambda i, j: (i, 0)),
        scratch_shapes=[pltpu.VMEM((tm, d), BF16), pltpu.VMEM((tm, d), F32)],
        compiler_params=_params("parallel", "arbitrary"),
        name="ffn",
    )(h2d, g.reshape(1, d), w1, w3, w2)


def _s5_discretize_kernel(are_ref, aim_ref, logdt_ref, bre_ref, bim_ref,
                          abre_ref, abim_ref, bcre_ref, bcim_ref):
    a_re, a_im = are_ref[...], aim_ref[...]
    dt = jnp.exp(logdt_ref[...])
    mag = jnp.exp(a_re * dt)
    ab_re = mag * jnp.cos(a_im * dt)
    ab_im = mag * jnp.sin(a_im * dt)
    e_re, e_im = ab_re - 1.0, ab_im
    den = a_re * a_re + a_im * a_im
    coef_re = (e_re * a_re + e_im * a_im) / den
    coef_im = (e_im * a_re - e_re * a_im) / den
    abre_ref[...] = ab_re
    abim_ref[...] = ab_im
    b_re, b_im = bre_ref[...], bim_ref[...]
    c_re, c_im = coef_re[:, None, :], coef_im[:, None, :]
    bcre_ref[...] = c_re * b_re - c_im * b_im
    bcim_ref[...] = c_re * b_im + c_im * b_re


def _s5_discretize(a_re, a_im, log_dt, b_re, b_im):
    g, n = a_re.shape
    c = b_re.shape[2]
    b_re_t = jnp.transpose(b_re, (0, 2, 1))
    b_im_t = jnp.transpose(b_im, (0, 2, 1))
    return pl.pallas_call(
        _s5_discretize_kernel,
        out_shape=(jax.ShapeDtypeStruct((g, n), F32), jax.ShapeDtypeStruct((g, n), F32),
                   jax.ShapeDtypeStruct((g, c, n), F32), jax.ShapeDtypeStruct((g, c, n), F32)),
        name="s5_discretize",
    )(a_re, a_im, log_dt.reshape(g, 1), b_re_t, b_im_t)


def _block_diag_halves(blocks):
    g, r, c = blocks.shape
    hg = g // 2
    eye = jnp.eye(hg, dtype=blocks.dtype)
    b = blocks.reshape(2, hg, r, c)
    return jnp.einsum('hgrc,gk->hgrkc', b, eye).reshape(2, hg * r, hg * c)


def _s5_kernel(u_ref, wbre_ref, wbim_ref, abre_ref, abim_ref, wcre_ref, wcim_ref,
               d_ref, gw_ref, gb_ref, o_ref, xre_ref, xim_ref, hre_ref, him_ref):
    @pl.when(pl.program_id(0) == 0)
    def _():
        hre_ref[...] = jnp.zeros_like(hre_ref)
        him_ref[...] = jnp.zeros_like(him_ref)

    rows = u_ref.shape[0]
    n_steps = rows // SUBLANES
    u = u_ref[...]
    u_bf = u.astype(BF16)
    hs = S5_HALF_STATES
    y_halves = []
    for half in range(2):
        uh = u_bf[:, half * S5_HALF_CH:(half + 1) * S5_HALF_CH]
        xre_ref[...] = _dot(uh, wbre_ref[half])
        xim_ref[...] = _dot(uh, wbim_ref[half])
        a_re = jnp.broadcast_to(abre_ref[:, half * hs:(half + 1) * hs], (SUBLANES, hs))
        a_im = jnp.broadcast_to(abim_ref[:, half * hs:(half + 1) * hs], (SUBLANES, hs))

        def step(t, carry):
            h_re, h_im = carry
            r0 = pl.multiple_of(t * SUBLANES, SUBLANES)
            n_re = a_re * h_re - a_im * h_im + xre_ref[pl.ds(r0, SUBLANES), :]
            n_im = a_re * h_im + a_im * h_re + xim_ref[pl.ds(r0, SUBLANES), :]
            xre_ref[pl.ds(r0, SUBLANES), :] = n_re
            xim_ref[pl.ds(r0, SUBLANES), :] = n_im
            return n_re, n_im

        h0 = (hre_ref[:, half * hs:(half + 1) * hs], him_ref[:, half * hs:(half + 1) * hs])
        h_re, h_im = lax.fori_loop(0, n_steps, step, h0, unroll=4)
        hre_ref[:, half * hs:(half + 1) * hs] = h_re
        him_ref[:, half * hs:(half + 1) * hs] = h_im
        y_halves.append(_dot(xre_ref[...].astype(BF16), wcre_ref[half])
                        - _dot(xim_ref[...].astype(BF16), wcim_ref[half]))
    y = jnp.concatenate(y_halves, axis=-1) + d_ref[...] * u
    y = _gelu_tanh(y)
    gate = _sigmoid(_dot(y.astype(BF16), gw_ref[...]) + gb_ref[...])
    o_ref[...] = y * gate


def _s5_mixer(u_tb, batch, seq, wb_re, wb_im, ab_re, ab_im, wc_re, wc_im, d_skip, glu_w, glu_b):
    rows = S5_TIME_BLOCK * batch
    width = S5_WIDTH
    return pl.pallas_call(
        _s5_kernel,
        out_shape=jax.ShapeDtypeStruct((seq * batch, width), F32),
        grid=(seq // S5_TIME_BLOCK,),
        in_specs=[pl.BlockSpec((rows, width), lambda i: (i, 0))] + [_vmem_spec()] * 9,
        out_specs=pl.BlockSpec((rows, width), lambda i: (i, 0)),
        scratch_shapes=[pltpu.VMEM((rows, S5_HALF_STATES), F32),
                        pltpu.VMEM((rows, S5_HALF_STATES), F32),
                        pltpu.VMEM((batch, 2 * S5_HALF_STATES), F32),
                        pltpu.VMEM((batch, 2 * S5_HALF_STATES), F32)],
        compiler_params=_params("arbitrary"),
        name="s5_mixer",
    )(u_tb, wb_re, wb_im, ab_re, ab_im, wc_re, wc_im, d_skip, glu_w, glu_b)


def _ssd_kernel(zs_ref, xbc_ref, tail_ref, dtraw_ref, convw_ref, convb_ref, dtb_ref, alog_ref, dexp_ref,
                normw_ref, tril_ref, expand_ref, o_ref, state_ref):
    q = SSD_CHUNK
    p = SSD_HEADDIM
    first = pl.program_id(1) == 0

    @pl.when(first)
    def _():
        state_ref[...] = jnp.zeros_like(state_ref)

    x = xbc_ref[...]
    tail = tail_ref[...] * jnp.where(first, 0.0, 1.0)
    row8 = lax.broadcasted_iota(jnp.int32, (SUBLANES, SSD_CONV_CH), 0)
    conv = convb_ref[...] + convw_ref[SSD_CONV - 1:SSD_CONV, :] * x
    for k in range(SSD_CONV - 1):
        back = SSD_CONV - 1 - k
        rolled = pltpu.roll(x, back, 0)
        top = jnp.where(row8 < back, pltpu.roll(tail, back, 0), rolled[0:SUBLANES])
        shifted = jnp.concatenate([top, rolled[SUBLANES:]], axis=0)
        conv = conv + convw_ref[k:k + 1, :] * shifted
    act = _silu(conv)
    xs = act[:, :SSD_WIDTH]
    b_bf = act[:, SSD_WIDTH:SSD_WIDTH + SSD_BC_WIDTH].astype(BF16)
    c_bf = act[:, SSD_WIDTH + SSD_BC_WIDTH:].astype(BF16)

    dt = _softplus(dtraw_ref[...] + dtb_ref[...])
    dt_a = dt * (-jnp.exp(alog_ref[...]) * LOG2E)
    hp = dt.shape[1]
    cs3 = _dot(tril_ref[...], jnp.concatenate(_split3(dt_a), axis=1))
    cs = cs3[:, :hp] + cs3[:, hp:2 * hp] + cs3[:, 2 * hp:]
    cs_t = cs.T
    cs_e = _dot(jnp.concatenate(_split3(cs), axis=1), expand_ref[...])
    dt_hi = dt.astype(BF16)
    dt_lo = (dt - dt_hi.astype(F32)).astype(BF16)
    dt_e = _dot(jnp.concatenate([dt_hi, dt_lo], axis=1), expand_ref[0:2 * hp, :])
    xdt = xs * dt_e
    xdt_bf = xdt.astype(BF16)
    cs_last_e = cs_e[q - 1:q, :]
    decay_in = jnp.exp2(cs_e)
    chunk_decay = jnp.exp2(cs_last_e)
    x_end = (xdt * jnp.exp2(cs_last_e - cs_e)).astype(BF16)

    row_i = lax.broadcasted_iota(jnp.int32, (q, q), 0)
    col_i = lax.broadcasted_iota(jnp.int32, (q, q), 1)
    causal = row_i >= col_i
    first_of_pair = lax.broadcasted_iota(jnp.int32, (q, 2 * p), 1) < p

    y_groups = []
    for g in range(SSD_GROUPS):
        bg = b_bf[:, g * SSD_STATE:(g + 1) * SSD_STATE]
        cg = c_bf[:, g * SSD_STATE:(g + 1) * SSD_STATE]
        gsl = slice(g * SSD_GROUP_WIDTH, (g + 1) * SSD_GROUP_WIDTH)
        cb = _dot_nt(cg, bg)
        state = state_ref[g]
        y_off = _dot(cg, state.astype(BF16)) * decay_in[:, gsl]
        y_pairs = []
        for pr in range(SSD_HPG // 2):
            h0 = g * SSD_HPG + 2 * pr
            xp = xdt_bf[:, h0 * p:(h0 + 2) * p]
            parts = []
            for h in (h0, h0 + 1):
                seg = jnp.broadcast_to(cs[:, h:h + 1], (q, q)) - jnp.broadcast_to(cs_t[h:h + 1, :], (q, q))
                decay = jnp.exp2(jnp.where(causal, seg, NEG_BIG))
                parts.append(_dot((cb * decay).astype(BF16), xp))
            y_pairs.append(jnp.where(first_of_pair, parts[0], parts[1]))
        y_groups.append(jnp.concatenate(y_pairs, axis=-1) + y_off)
        new_state = _dot_tn(bg, x_end[:, gsl])
        state_ref[g] = state * chunk_decay[:, gsl] + new_state

    y = jnp.concatenate(y_groups, axis=-1) + dexp_ref[...] * xs
    y = y * zs_ref[...]
    outs = []
    for g in range(SSD_GROUPS):
        gsl = slice(g * SSD_GROUP_WIDTH, (g + 1) * SSD_GROUP_WIDTH)
        yg = y[:, gsl]
        ms = jnp.mean(yg * yg, axis=-1, keepdims=True)
        outs.append(yg * lax.rsqrt(ms + RMS_EPS) * normw_ref[:, gsl])
    o_ref[...] = jnp.concatenate(outs, axis=-1)


def _ssd_mixer(zs, xbc, dt_raw, batch, seq, conv_w, conv_b, dt_bias, a_log, d_skip, norm_w):
    q = SSD_CHUNK
    n_c = seq // q
    heads_pad = LANES
    pad_h = heads_pad - SSD_HEADS
    dtb = jnp.pad(dt_bias, (0, pad_h)).reshape(1, heads_pad)
    alog = jnp.pad(a_log, (0, pad_h)).reshape(1, heads_pad)
    d_exp = jnp.repeat(d_skip, SSD_HEADDIM).reshape(1, SSD_WIDTH)
    t = jnp.arange(q)
    tril = (t[:, None] >= t[None, :]).astype(BF16)
    expand = (jnp.arange(heads_pad)[:, None] == (jnp.arange(SSD_WIDTH) // SSD_HEADDIM)[None, :]).astype(BF16)
    expand3 = jnp.tile(expand, (3, 1))
    row_map = lambda b, c: (b * n_c + c, 0)
    tiles_per_chunk = q // SUBLANES
    tail_map = lambda b, c: (jnp.maximum((b * n_c + c) * tiles_per_chunk - 1, 0), 0)
    return pl.pallas_call(
        _ssd_kernel,
        out_shape=jax.ShapeDtypeStruct((batch * seq, SSD_WIDTH), F32),
        grid=(batch, n_c),
        in_specs=[pl.BlockSpec((q, SSD_WIDTH), row_map),
                  pl.BlockSpec((q, SSD_CONV_CH), row_map),
                  pl.BlockSpec((SUBLANES, SSD_CONV_CH), tail_map),
                  pl.BlockSpec((q, heads_pad), row_map)] + [_vmem_spec()] * 8,
        out_specs=pl.BlockSpec((q, SSD_WIDTH), row_map),
        scratch_shapes=[pltpu.VMEM((SSD_GROUPS, SSD_STATE, SSD_GROUP_WIDTH), F32)],
        compiler_params=_params("parallel", "arbitrary"),
        name="ssd_mixer",
    )(zs, xbc, xbc, dt_raw, conv_w, conv_b.reshape(1, SSD_CONV_CH), dtb, alog, d_exp,
      norm_w.reshape(1, SSD_WIDTH), tril, expand3)


def _head_rmsnorm(x, w_tiled, seg_ones, head_dim):
    sq = x * x
    hi = sq.astype(BF16)
    lo = (sq - hi.astype(F32)).astype(BF16)
    width = x.shape[1]
    blk = seg_ones.shape[0]
    parts = []
    for c0 in range(0, width, blk):
        parts.append(_dot(hi[:, c0:c0 + blk], seg_ones) + _dot(lo[:, c0:c0 + blk], seg_ones))
    ss = parts[0] if len(parts) == 1 else jnp.concatenate(parts, axis=-1)
    return x * lax.rsqrt(ss * (1.0 / head_dim) + RMS_EPS) * w_tiled


def _swa_kernel(q_ref, k_ref, v_ref, kprev_ref, vprev_ref, qw_ref, kw_ref, sink_ref, seg_ref, rep_ref, o_ref):
    blk = SWA_WINDOW
    hd = SWA_HEADDIM
    slab = SWA_HPG * hd
    first = pl.program_id(1) == 0

    n_blk = q_ref.shape[0] // blk
    seg_ones = seg_ref[...]
    qn = _head_rmsnorm(q_ref[...], qw_ref[...], seg_ones, hd) * (hd ** -0.5)
    k_both = jnp.concatenate([kprev_ref[...], k_ref[...]], axis=0)
    k_cat = _head_rmsnorm(k_both, kw_ref[...], seg_ones, hd).astype(BF16)
    v_cat = jnp.concatenate([vprev_ref[...], v_ref[...]], axis=0).astype(BF16)

    qpos = lax.broadcasted_iota(jnp.int32, (blk, blk), 0)
    kpos = lax.broadcasted_iota(jnp.int32, (blk, blk), 1)
    in_prev = kpos > qpos
    prev_bias = jnp.where(first, NEG_BIG, 0.0)
    lane = lax.broadcasted_iota(jnp.int32, (blk, slab), 1)
    head_masks = [(lane >= r * hd) & (lane < (r + 1) * hd) for r in range(SWA_HPG)]

    scores, v_reps = {}, []
    for g in range(SWA_KV_HEADS):
        rep = rep_ref[g]
        k_rep = _dot(k_cat, rep).astype(BF16)
        v_reps.append(_dot(v_cat, rep).astype(BF16))
        for j in range(n_blk):
            q_slab = qn[j * blk:(j + 1) * blk, g * slab:(g + 1) * slab]
            q_stack = jnp.concatenate([jnp.where(mk, q_slab, 0.0) for mk in head_masks], axis=0).astype(BF16)
            scores[g, j] = _dot_nt(q_stack, k_rep[j * blk:(j + 2) * blk])

    probs = {}
    for g in range(SWA_KV_HEADS):
        for j in range(n_blk):
            p_heads = []
            for r in range(SWA_HPG):
                s = scores[g, j][r * blk:(r + 1) * blk, :]
                s_prev = s[:, :blk] + prev_bias if j == 0 else s[:, :blk]
                s_fold = jnp.where(in_prev, s_prev, s[:, blk:])
                sink = sink_ref[g * SWA_HPG + r]
                m = jnp.maximum(jnp.max(s_fold, axis=-1, keepdims=True), sink)
                p = jnp.exp(s_fold - m)
                denom = jnp.sum(p, axis=-1, keepdims=True) + jnp.exp(sink - m)
                pn = p * (1.0 / denom)
                p_heads.append(jnp.concatenate([jnp.where(in_prev, pn, 0.0).astype(BF16),
                                                jnp.where(in_prev, 0.0, pn).astype(BF16)], axis=1))
            probs[g, j] = jnp.concatenate(p_heads, axis=0)

    for g in range(SWA_KV_HEADS):
        for j in range(n_blk):
            o_stack = _dot(probs[g, j], v_reps[g][j * blk:(j + 2) * blk])
            o_slab = jnp.where(head_masks[0], o_stack[0:blk], 0.0)
            for r in range(1, SWA_HPG):
                o_slab = o_slab + jnp.where(head_masks[r], o_stack[r * blk:(r + 1) * blk], 0.0)
            o_ref[j * blk:(j + 1) * blk, g * slab:(g + 1) * slab] = o_slab


def _swa_mixer(q, k, v, batch, seq, q_norm_w, k_norm_w, sinks):
    blk = SWA_WINDOW
    n_blk = SWA_BLOCKS_PER_STEP
    tq = n_blk * blk
    n_b = seq // tq
    qw = jnp.tile(q_norm_w, SWA_HEADS).reshape(1, SWA_WIDTH)
    kw = jnp.tile(k_norm_w, SWA_KV_HEADS).reshape(1, SWA_KV_WIDTH)
    seg = SWA_KV_WIDTH
    lane = jnp.arange(seg)
    seg_ones = (lane[:, None] // SWA_HEADDIM == lane[None, :] // SWA_HEADDIM).astype(BF16)
    rep = (lane[None, :, None] == (jnp.arange(SWA_KV_HEADS)[:, None, None] * SWA_HEADDIM
                                   + lane[None, None, :] % SWA_HEADDIM)).astype(BF16)
    row_map = lambda b, i: (b * n_b + i, 0)
    prev_map = lambda b, i: (jnp.maximum((b * n_b + i) * n_blk - 1, 0), 0)
    return pl.pallas_call(
        _swa_kernel,
        out_shape=jax.ShapeDtypeStruct((batch * seq, SWA_WIDTH), F32),
        grid=(batch, n_b),
        in_specs=[pl.BlockSpec((tq, SWA_WIDTH), row_map),
                  pl.BlockSpec((tq, SWA_KV_WIDTH), row_map),
                  pl.BlockSpec((tq, SWA_KV_WIDTH), row_map),
                  pl.BlockSpec((blk, SWA_KV_WIDTH), prev_map),
                  pl.BlockSpec((blk, SWA_KV_WIDTH), prev_map),
                  _vmem_spec(), _vmem_spec(),
                  pl.BlockSpec(memory_space=pltpu.SMEM),
                  _vmem_spec(), _vmem_spec()],
        out_specs=pl.BlockSpec((tq, SWA_WIDTH), row_map),
        compiler_params=_params("parallel", "parallel"),
        name="swa_mixer",
    )(q, k, v, k, v, qw, kw, sinks, seg_ones, rep)


def _gla_kernel(q_ref, k_ref, v_ref, gate_ref, glr_ref, w2_ref, gb_ref, normw_ref, tril_ref,
                o_ref, state_ref):
    c = GLA_CHUNK
    dk, dv = GLA_HEAD_K, GLA_HEAD_V
    rows = q_ref.shape[0]

    @pl.when(pl.program_id(1) == 0)
    def _():
        state_ref[...] = jnp.zeros_like(state_ref)

    w2 = w2_ref[...]
    g_hi, g_mid, g_lo = _split3(glr_ref[...])
    w_hi, w_mid, w_lo = _split3(w2)
    pre = (_dot(g_hi, w_hi) + _dot(g_hi, w_mid) + _dot(g_mid, w_hi)
           + _dot(g_hi, w_lo) + _dot(g_mid, w_mid) + _dot(g_lo, w_hi)) + gb_ref[...]
    gk = -_softplus(-pre) * (LOG2E / GLA_GATE_NORMALIZER)
    gk3 = _dot(tril_ref[...], jnp.concatenate(_split3(gk), axis=1))
    kw = gk.shape[1]
    bc = gk3[:, :kw] + gk3[:, kw:2 * kw] + gk3[:, 2 * kw:]

    n_chunks = rows // c
    pair = 2 * c
    assert rows % pair == 0
    b_lasts = [bc[(i + 1) * c - 1:(i + 1) * c, :] for i in range(n_chunks)]
    b_last = jnp.concatenate([jnp.broadcast_to(b, (c, kw)) for b in b_lasts], axis=0)
    k_all = k_ref[...]
    q_t = (q_ref[...] * (dk ** -0.5) * jnp.exp2(bc)).astype(BF16)
    k_t = (k_all * jnp.exp2(-bc)).astype(BF16)
    k_end = (k_all * jnp.exp2(b_last - bc)).astype(BF16)
    decs = [jnp.exp2(b) for b in b_lasts]
    v_bf = v_ref[...].astype(BF16)
    row_i = lax.broadcasted_iota(jnp.int32, (pair, pair), 0)
    col_i = lax.broadcasted_iota(jnp.int32, (pair, pair), 1)
    same_chunk_causal = (row_i >= col_i) & (col_i >= (row_i // c) * c)

    heads = []
    for h in range(GLA_HEADS):
        ksl = slice(h * dk, (h + 1) * dk)
        vsl = slice(h * dv, (h + 1) * dv)
        o_intra = []
        for r0 in range(0, rows, pair):
            rsl = slice(r0, r0 + pair)
            att = jnp.where(same_chunk_causal, _dot_nt(q_t[rsl, ksl], k_t[rsl, ksl]), 0.0).astype(BF16)
            o_intra.append(_dot(att, v_bf[rsl, vsl]))
        state = state_ref[h]
        o_inter = []
        for i in range(n_chunks):
            rsl = slice(i * c, (i + 1) * c)
            o_inter.append(_dot_nt(q_t[rsl, ksl], state.astype(BF16)))
            state = state * decs[i][:, ksl] + _dot_tn(v_bf[rsl, vsl], k_end[rsl, ksl])
        state_ref[h] = state
        o = jnp.concatenate(o_intra, axis=0) + jnp.concatenate(o_inter, axis=0)
        ms = jnp.mean(o * o, axis=-1, keepdims=True)
        heads.append(o * lax.rsqrt(ms + RMS_EPS) * normw_ref[...])
    o_ref[...] = jnp.concatenate(heads, axis=-1) * gate_ref[...]


def _gla_mixer(q, k, v, gate, g_lr, batch, seq, gate_w2, gate_b, norm_w):
    tq = GLA_TIME_BLOCK
    n_t = seq // tq
    rank_pad = LANES
    w2 = jnp.pad(gate_w2, ((0, rank_pad - GLA_GATE_RANK), (0, 0)))
    idx = jnp.arange(tq)
    tril = ((idx[:, None] >= idx[None, :]) & (idx[:, None] // GLA_CHUNK == idx[None, :] // GLA_CHUNK)).astype(BF16)
    row_map = lambda b, i: (b * n_t + i, 0)
    return pl.pallas_call(
        _gla_kernel,
        out_shape=jax.ShapeDtypeStruct((batch * seq, GLA_VALUE_WIDTH), F32),
        grid=(batch, n_t),
        in_specs=[pl.BlockSpec((tq, GLA_KEY_WIDTH), row_map),
                  pl.BlockSpec((tq, GLA_KEY_WIDTH), row_map),
                  pl.BlockSpec((tq, GLA_VALUE_WIDTH), row_map),
                  pl.BlockSpec((tq, GLA_VALUE_WIDTH), row_map),
                  pl.BlockSpec((tq, rank_pad), row_map)] + [_vmem_spec()] * 4,
        out_specs=pl.BlockSpec((tq, GLA_VALUE_WIDTH), row_map),
        scratch_shapes=[pltpu.VMEM((GLA_HEADS, GLA_HEAD_V, GLA_HEAD_K), F32)],
        compiler_params=_params("parallel", "arbitrary"),
        name="gla_mixer",
    )(q, k, v, gate, g_lr, w2, gate_b.reshape(1, GLA_KEY_WIDTH), norm_w.reshape(1, GLA_HEAD_V), tril)


def _pad_cols(w, total):
    return jnp.pad(w, ((0, 0), (0, total - w.shape[1])))


def _layer0(h2d, batch, seq, norm_mix, w_in, s5_A_re, s5_A_im, s5_log_dt, s5_B_re, s5_B_im, s5_C_re,
            s5_C_im, s5_D, s5_glu_w, s5_glu_b, ssd_conv_w, ssd_conv_b, ssd_dt_bias, ssd_A_log,
            ssd_D, ssd_norm_w, w_out, norm_ffn, w1, w3, w2):
    z0 = S5_WIDTH
    x0 = z0 + SSD_WIDTH
    dt0 = x0 + SSD_CONV_CH
    pieces = [(0, S5_WIDTH, PLAIN), (z0, SSD_WIDTH, SILU), (x0, SSD_CONV_CH, PLAIN), (dt0, LANES, PLAIN)]
    w_in_p = _pad_cols(w_in, dt0 + LANES).astype(BF16)
    u, zs, xbc, dt_raw = _norm_proj(h2d, norm_mix, w_in_p, pieces, 0, batch, seq)

    ab_re, ab_im, bc_re, bc_im = _s5_discretize(s5_A_re, s5_A_im, s5_log_dt, s5_B_re, s5_B_im)
    wb_re = _block_diag_halves(bc_re).astype(BF16)
    wb_im = _block_diag_halves(bc_im).astype(BF16)
    wc_re = _block_diag_halves(jnp.transpose(s5_C_re, (0, 2, 1))).astype(BF16)
    wc_im = _block_diag_halves(jnp.transpose(s5_C_im, (0, 2, 1))).astype(BF16)
    ya = _s5_mixer(u.reshape(seq * batch, S5_WIDTH), batch, seq, wb_re, wb_im,
                   ab_re.reshape(1, -1), ab_im.reshape(1, -1), wc_re, wc_im,
                   s5_D.reshape(1, S5_WIDTH), s5_glu_w.astype(BF16), s5_glu_b.reshape(1, S5_WIDTH))
    ya = ya.reshape(seq, batch * S5_WIDTH)

    yb = _ssd_mixer(zs, xbc, dt_raw, batch, seq, ssd_conv_w, ssd_conv_b, ssd_dt_bias, ssd_A_log,
                    ssd_D, ssd_norm_w)
    h2d = _out_proj([ya, yb], [True, False], h2d, w_out.astype(BF16), batch, seq)
    return _ffn(h2d, norm_ffn, w1.astype(BF16), w3.astype(BF16), w2.astype(BF16))


def _layer1(h2d, batch, seq, norm_mix, w_in, swa_q_norm, swa_k_norm, swa_sinks, gla_gate_w2,
            gla_gate_b, gla_norm_w, w_out, norm_ffn, w1, w3, w2):
    widths = (SWA_WIDTH, SWA_KV_WIDTH, SWA_KV_WIDTH, GLA_KEY_WIDTH, GLA_KEY_WIDTH,
              GLA_VALUE_WIDTH, GLA_VALUE_WIDTH, LANES)
    starts = np.concatenate([[0], np.cumsum(widths)[:-1]]).tolist()
    kinds = (PLAIN,) * 6 + (SILU, PLAIN)
    pieces = list(zip(starts, widths, kinds))
    w_in_p = _pad_cols(w_in, starts[-1] + LANES).astype(BF16)
    qc, kc, vc, qd, kd, vd, gate, g_lr = _norm_proj(h2d, norm_mix, w_in_p, pieces, None, batch, seq)
    yc = _swa_mixer(qc, kc, vc, batch, seq, swa_q_norm, swa_k_norm, swa_sinks)
    yd = _gla_mixer(qd, kd, vd, gate, g_lr, batch, seq, gla_gate_w2, gla_gate_b, gla_norm_w)
    h2d = _out_proj([yc, yd], [False, False], h2d, w_out.astype(BF16), batch, seq)
    return _ffn(h2d, norm_ffn, w1.astype(BF16), w3.astype(BF16), w2.astype(BF16))


def kernel(x, norm0_mix, w_in0, s5_A_re, s5_A_im, s5_log_dt, s5_B_re, s5_B_im, s5_C_re, s5_C_im, s5_D, s5_glu_w, s5_glu_b, ssd_conv_w, ssd_conv_b, ssd_dt_bias, ssd_A_log, ssd_D, ssd_norm_w, w_out0, norm0_ffn, ffn0_w1, ffn0_w3, ffn0_w2, norm1_mix, w_in1, swa_q_norm, swa_k_norm, swa_sinks, gla_gate_w2, gla_gate_b, gla_norm_w, w_out1, norm1_ffn, ffn1_w1, ffn1_w3, ffn1_w2):
    batch, seq, d = x.shape
    assert batch == SUBLANES and d == D_MODEL and seq % OUT_ROW_BLOCK == 0
    h2d = x.reshape(batch * seq, d)
    h2d = _layer0(h2d, batch, seq, norm0_mix, w_in0, s5_A_re, s5_A_im, s5_log_dt, s5_B_re, s5_B_im,
                  s5_C_re, s5_C_im, s5_D, s5_glu_w, s5_glu_b, ssd_conv_w, ssd_conv_b, ssd_dt_bias,
                  ssd_A_log, ssd_D, ssd_norm_w, w_out0, norm0_ffn, ffn0_w1, ffn0_w3, ffn0_w2)
    h2d = _layer1(h2d, batch, seq, norm1_mix, w_in1, swa_q_norm, swa_k_norm, swa_sinks, gla_gate_w2,
                  gla_gate_b, gla_norm_w, w_out1, norm1_ffn, ffn1_w1, ffn1_w3, ffn1_w2)
    return h2d.reshape(batch, seq, d)
```

```python
import functools

import numpy as np
import jax
import jax.numpy as jnp
from jax import lax
from jax.experimental import pallas as pl
from jax.experimental.pallas import tpu as pltpu

F32 = jnp.float32
BF16 = jnp.bfloat16

LANES = 128
SUBLANES = 8
VMEM_LIMIT_BYTES = 56 * 1024 * 1024

D_MODEL = 2048
RMS_EPS = 1e-6
MIX_WIDTH = D_MODEL

S5_WIDTH = D_MODEL // 4
S5_GROUP_CH = 16
S5_GROUPS = S5_WIDTH // S5_GROUP_CH
S5_STATE = 64
S5_HALF_CH = S5_WIDTH // 2
S5_HALF_GROUPS = S5_GROUPS // 2
S5_HALF_STATES = S5_HALF_GROUPS * S5_STATE
S5_TIME_BLOCK = 128

SSD_WIDTH = MIX_WIDTH - S5_WIDTH
SSD_HEADDIM = 64
SSD_HEADS = SSD_WIDTH // SSD_HEADDIM
SSD_GROUPS = 4
SSD_HPG = SSD_HEADS // SSD_GROUPS
SSD_STATE = 128
SSD_CONV = 4
SSD_CHUNK = 128
SSD_BC_WIDTH = SSD_GROUPS * SSD_STATE
SSD_CONV_CH = SSD_WIDTH + 2 * SSD_BC_WIDTH
SSD_GROUP_WIDTH = SSD_WIDTH // SSD_GROUPS

SWA_HEADDIM = 64
SWA_WIDTH = MIX_WIDTH // 2
SWA_HEADS = SWA_WIDTH // SWA_HEADDIM
SWA_KV_HEADS = 4
SWA_HPG = SWA_HEADS // SWA_KV_HEADS
SWA_WINDOW = 128
SWA_KV_WIDTH = SWA_KV_HEADS * SWA_HEADDIM
SWA_BLOCKS_PER_STEP = 4

GLA_VALUE_WIDTH = MIX_WIDTH - SWA_WIDTH
GLA_KEY_WIDTH = GLA_VALUE_WIDTH // 2
GLA_HEADS = 4
GLA_HEAD_K = GLA_KEY_WIDTH // GLA_HEADS
GLA_HEAD_V = GLA_VALUE_WIDTH // GLA_HEADS
GLA_GATE_RANK = 16
GLA_GATE_NORMALIZER = 16.0
GLA_CHUNK = 64
GLA_TIME_BLOCK = 256

FFN_HIDDEN = -(-(8 * D_MODEL) // (3 * 256)) * 256

PROJ_ROW_BLOCK = 512
FFN_ROW_BLOCK = 512
FFN_HIDDEN_BLOCK = 512
OUT_ROW_BLOCK = 512
PROJ_COL_CHUNK = 512

NEG_BIG = -1e30


def _params(*semantics):
    return pltpu.CompilerParams(dimension_semantics=semantics, vmem_limit_bytes=VMEM_LIMIT_BYTES)


def _vmem_spec():
    return pl.BlockSpec(memory_space=pltpu.VMEM)


def _dot(a, b):
    return jnp.dot(a, b, preferred_element_type=F32)


def _dot_nt(a, b):
    return lax.dot_general(a, b, (((1,), (1,)), ((), ())), preferred_element_type=F32)


def _dot_tn(a, b):
    return lax.dot_general(a, b, (((0,), (0,)), ((), ())), preferred_element_type=F32)


def _split3(v):
    hi = v.astype(BF16)
    r1 = v - hi.astype(F32)
    mid = r1.astype(BF16)
    lo = (r1 - mid.astype(F32)).astype(BF16)
    return hi, mid, lo


def _dot_exact_rhs01(v, m01):
    hi, mid, lo = _split3(v)
    return _dot(hi, m01) + _dot(mid, m01) + _dot(lo, m01)


def _dot_exact_lhs01(m01, v):
    hi, mid, lo = _split3(v)
    return _dot(m01, hi) + _dot(m01, mid) + _dot(m01, lo)


LOG2E = float(np.log2(np.e))


def _sigmoid(x):
    return 1.0 / (1.0 + jnp.exp2(x * (-LOG2E)))


def _silu(x):
    return x * _sigmoid(x)


def _softplus(x):
    return jnp.maximum(x, 0.0) + jnp.log1p(jnp.exp(-jnp.abs(x)))


def _gelu_tanh(x):
    c = np.sqrt(2.0 / np.pi).astype(np.float32)
    return 0.5 * x * (1.0 + jnp.tanh(c * (x + 0.044715 * (x * x * x))))


def _rmsnorm_rows(x, w):
    ms = jnp.mean(x * x, axis=-1, keepdims=True)
    return x * lax.rsqrt(ms + RMS_EPS) * w


PLAIN, SILU = "plain", "silu"


def _norm_proj_kernel(pieces, x_ref, g_ref, w_ref, *rest):
    out_refs = rest[:len(pieces)]
    xn_ref = rest[len(pieces)]
    xn_ref[...] = _rmsnorm_rows(x_ref[...], g_ref[...]).astype(BF16)
    for (start, width, kind), o_ref in zip(pieces, out_refs):
        for c0 in range(0, width, PROJ_COL_CHUNK):
            cw = min(PROJ_COL_CHUNK, width - c0)
            val = _dot(xn_ref[...], w_ref[:, start + c0:start + c0 + cw])
            o_ref[:, c0:c0 + cw] = _silu(val) if kind == SILU else val


def _norm_proj(x2d, g, w_bf16, pieces, time_major_piece, batch, seq):
    m, d = x2d.shape
    tm = PROJ_ROW_BLOCK
    n_l = seq // tm
    out_shapes, out_specs = [], []
    for idx, (_, width, _) in enumerate(pieces):
        if idx == time_major_piece:
            out_shapes.append(jax.ShapeDtypeStruct((seq, batch * width), F32))
            out_specs.append(pl.BlockSpec((tm, width), lambda i: (i % n_l, i // n_l)))
        else:
            out_shapes.append(jax.ShapeDtypeStruct((m, width), F32))
            out_specs.append(pl.BlockSpec((tm, width), lambda i: (i, 0)))
    return pl.pallas_call(
        functools.partial(_norm_proj_kernel, tuple(pieces)),
        out_shape=tuple(out_shapes),
        grid=(m // tm,),
        in_specs=[pl.BlockSpec((tm, d), lambda i: (i, 0)),
                  pl.BlockSpec((1, d), lambda i: (0, 0)),
                  _vmem_spec()],
        out_specs=tuple(out_specs),
        scratch_shapes=[pltpu.VMEM((tm, d), BF16)],
        compiler_params=_params("parallel"),
        name="norm_proj",
    )(x2d, g.reshape(1, d), w_bf16)


def _out_proj_kernel(splits, *refs):
    n = len(splits)
    y_refs, h_ref, w_ref, o_ref = refs[:n], refs[n], refs[n + 1], refs[n + 2]
    acc = h_ref[...]
    for (start, width), y_ref in zip(splits, y_refs):
        acc = acc + _dot(y_ref[...].astype(BF16), w_ref[start:start + width, :])
    o_ref[...] = acc


def _out_proj(ys, time_major, h2d, w_bf16, batch, seq):
    m, d = h2d.shape
    tm = OUT_ROW_BLOCK
    n_l = seq // tm
    splits, in_specs, start = [], [], 0
    for y, tmaj in zip(ys, time_major):
        width = y.shape[1] // batch if tmaj else y.shape[1]
        splits.append((start, width))
        start += width
        if tmaj:
            in_specs.append(pl.BlockSpec((tm, width), lambda i: (i % n_l, i // n_l)))
        else:
            in_specs.append(pl.BlockSpec((tm, width), lambda i: (i, 0)))
    in_specs += [pl.BlockSpec((tm, d), lambda i: (i, 0)), _vmem_spec()]
    return pl.pallas_call(
        functools.partial(_out_proj_kernel, tuple(splits)),
        out_shape=jax.ShapeDtypeStruct((m, d), F32),
        grid=(m // tm,),
        in_specs=in_specs,
        out_specs=pl.BlockSpec((tm, d), lambda i: (i, 0)),
        compiler_params=_params("parallel"),
        name="out_proj",
    )(*ys, h2d, w_bf16)


def _ffn_kernel(h_ref, g_ref, w1_ref, w3_ref, w2_ref, o_ref, xn_ref, acc_ref):
    j = pl.program_id(1)
    last = pl.num_programs(1) - 1

    def hidden_block():
        xn = xn_ref[...]
        a = _dot(xn, w1_ref[...])
        b = _dot(xn, w3_ref[...])
        return _dot((_silu(a) * b).astype(BF16), w2_ref[...])

    @pl.when(j == 0)
    def _():
        xn_ref[...] = _rmsnorm_rows(h_ref[...], g_ref[...]).astype(BF16)
        acc_ref[...] = hidden_block()

    @pl.when((j > 0) & (j < last))
    def _():
        acc_ref[...] += hidden_block()

    @pl.when(j == last)
    def _():
        o_ref[...] = h_ref[...] + acc_ref[...] + hidden_block()


def _ffn(h2d, g, w1, w3, w2):
    m, d = h2d.shape
    hidden = w1.shape[1]
    tm, th = FFN_ROW_BLOCK, FFN_HIDDEN_BLOCK
    return pl.pallas_call(
        _ffn_kernel,
        out_shape=jax.ShapeDtypeStruct((m, d), F32),
        grid=(m // tm, hidden // th),
        in_specs=[pl.BlockSpec((tm, d), lambda i, j: (i, 0)),
                  pl.BlockSpec((1, d), lambda i, j: (0, 0)),
                  pl.BlockSpec((d, th), lambda i, j: (0, j)),
                  pl.BlockSpec((d, th), lambda i, j: (0, j)),
                  pl.BlockSpec((th, d), lambda i, j: (j, 0))],
        out_specs=pl.BlockSpec((tm, d), lambda i, j: (i, 0)),
        scratch_shapes=[pltpu.VMEM((tm, d), BF16), pltpu.VMEM((tm, d), F32)],
        compiler_params=_params("parallel", "arbitrary"),
        name="ffn",
    )(h2d, g.reshape(1, d), w1, w3, w2)


def _s5_discretize_kernel(are_ref, aim_ref, logdt_ref, bre_ref, bim_ref,
                          abre_ref, abim_ref, bcre_ref, bcim_ref):
    a_re, a_im = are_ref[...], aim_ref[...]
    dt = jnp.exp(logdt_ref[...])
    mag = jnp.exp(a_re * dt)
    ab_re = mag * jnp.cos(a_im * dt)
    ab_im = mag * jnp.sin(a_im * dt)
    e_re, e_im = ab_re - 1.0, ab_im
    den = a_re * a_re + a_im * a_im
    coef_re = (e_re * a_re + e_im * a_im) / den
    coef_im = (e_im * a_re - e_re * a_im) / den
    abre_ref[...] = ab_re
    abim_ref[...] = ab_im
    b_re, b_im = bre_ref[...], bim_ref[...]
    c_re, c_im = coef_re[:, None, :], coef_im[:, None, :]
    bcre_ref[...] = c_re * b_re - c_im * b_im
    bcim_ref[...] = c_re * b_im + c_im * b_re


def _s5_discretize(a_re, a_im, log_dt, b_re, b_im):
    g, n = a_re.shape
    c = b_re.shape[2]
    b_re_t = jnp.transpose(b_re, (0, 2, 1))
    b_im_t = jnp.transpose(b_im, (0, 2, 1))
    return pl.pallas_call(
        _s5_discretize_kernel,
        out_shape=(jax.ShapeDtypeStruct((g, n), F32), jax.ShapeDtypeStruct((g, n), F32),
                   jax.ShapeDtypeStruct((g, c, n), F32), jax.ShapeDtypeStruct((g, c, n), F32)),
        name="s5_discretize",
    )(a_re, a_im, log_dt.reshape(g, 1), b_re_t, b_im_t)


def _block_diag_halves(blocks):
    g, r, c = blocks.shape
    hg = g // 2
    eye = jnp.eye(hg, dtype=blocks.dtype)
    b = blocks.reshape(2, hg, r, c)
    return jnp.einsum('hgrc,gk->hgrkc', b, eye).reshape(2, hg * r, hg * c)


def _s5_kernel(u_ref, wbre_ref, wbim_ref, abre_ref, abim_ref, wcre_ref, wcim_ref,
               d_ref, gw_ref, gb_ref, o_ref, xre_ref, xim_ref, hre_ref, him_ref):
    @pl.when(pl.program_id(0) == 0)
    def _():
        hre_ref[...] = jnp.zeros_like(hre_ref)
        him_ref[...] = jnp.zeros_like(him_ref)

    rows = u_ref.shape[0]
    n_steps = rows // SUBLANES
    u = u_ref[...]
    u_bf = u.astype(BF16)
    hs = S5_HALF_STATES
    y_halves = []
    for half in range(2):
        uh = u_bf[:, half * S5_HALF_CH:(half + 1) * S5_HALF_CH]
        xre_ref[...] = _dot(uh, wbre_ref[half])
        xim_ref[...] = _dot(uh, wbim_ref[half])
        a_re = jnp.broadcast_to(abre_ref[:, half * hs:(half + 1) * hs], (SUBLANES, hs))
        a_im = jnp.broadcast_to(abim_ref[:, half * hs:(half + 1) * hs], (SUBLANES, hs))

        def step(t, carry):
            h_re, h_im = carry
            r0 = pl.multiple_of(t * SUBLANES, SUBLANES)
            n_re = a_re * h_re - a_im * h_im + xre_ref[pl.ds(r0, SUBLANES), :]
            n_im = a_re * h_im + a_im * h_re + xim_ref[pl.ds(r0, SUBLANES), :]
            xre_ref[pl.ds(r0, SUBLANES), :] = n_re
            xim_ref[pl.ds(r0, SUBLANES), :] = n_im
            return n_re, n_im

        h0 = (hre_ref[:, half * hs:(half + 1) * hs], him_ref[:, half * hs:(half + 1) * hs])
        h_re, h_im = lax.fori_loop(0, n_steps, step, h0, unroll=4)
        hre_ref[:, half * hs:(half + 1) * hs] = h_re
        him_ref[:, half * hs:(half + 1) * hs] = h_im
        y_halves.append(_dot(xre_ref[...].astype(BF16), wcre_ref[half])
                        - _dot(xim_ref[...].astype(BF16), wcim_ref[half]))
    y = jnp.concatenate(y_halves, axis=-1) + d_ref[...] * u
    y = _gelu_tanh(y)
    gate = _sigmoid(_dot(y.astype(BF16), gw_ref[...]) + gb_ref[...])
    o_ref[...] = y * gate


def _s5_mixer(u_tb, batch, seq, wb_re, wb_im, ab_re, ab_im, wc_re, wc_im, d_skip, glu_w, glu_b):
    rows = S5_TIME_BLOCK * batch
    width = S5_WIDTH
    return pl.pallas_call(
        _s5_kernel,
        out_shape=jax.ShapeDtypeStruct((seq * batch, width), F32),
        grid=(seq // S5_TIME_BLOCK,),
        in_specs=[pl.BlockSpec((rows, width), lambda i: (i, 0))] + [_vmem_spec()] * 9,
        out_specs=pl.BlockSpec((rows, width), lambda i: (i, 0)),
        scratch_shapes=[pltpu.VMEM((rows, S5_HALF_STATES), F32),
                        pltpu.VMEM((rows, S5_HALF_STATES), F32),
                        pltpu.VMEM((batch, 2 * S5_HALF_STATES), F32),
                        pltpu.VMEM((batch, 2 * S5_HALF_STATES), F32)],
        compiler_params=_params("arbitrary"),
        name="s5_mixer",
    )(u_tb, wb_re, wb_im, ab_re, ab_im, wc_re, wc_im, d_skip, glu_w, glu_b)


def _ssd_kernel(zs_ref, xbc_ref, tail_ref, dtraw_ref, convw_ref, convb_ref, dtb_ref, alog_ref, dexp_ref,
                normw_ref, tril_ref, expand_ref, o_ref, state_ref):
    q = SSD_CHUNK
    p = SSD_HEADDIM
    first = pl.program_id(1) == 0

    @pl.when(first)
    def _():
        state_ref[...] = jnp.zeros_like(state_ref)

    x = xbc_ref[...]
    tail = tail_ref[...] * jnp.where(first, 0.0, 1.0)
    row8 = lax.broadcasted_iota(jnp.int32, (SUBLANES, SSD_CONV_CH), 0)
    conv = convb_ref[...] + convw_ref[SSD_CONV - 1:SSD_CONV, :] * x
    for k in range(SSD_CONV - 1):
        back = SSD_CONV - 1 - k
        rolled = pltpu.roll(x, back, 0)
        top = jnp.where(row8 < back, pltpu.roll(tail, back, 0), rolled[0:SUBLANES])
        shifted = jnp.concatenate([top, rolled[SUBLANES:]], axis=0)
        conv = conv + convw_ref[k:k + 1, :] * shifted
    act = _silu(conv)
    xs = act[:, :SSD_WIDTH]
    b_bf = act[:, SSD_WIDTH:SSD_WIDTH + SSD_BC_WIDTH].astype(BF16)
    c_bf = act[:, SSD_WIDTH + SSD_BC_WIDTH:].astype(BF16)

    dt = _softplus(dtraw_ref[...] + dtb_ref[...])
    dt_a = dt * (-jnp.exp(alog_ref[...]) * LOG2E)
    hp = dt.shape[1]
    cs3 = _dot(tril_ref[...], jnp.concatenate(_split3(dt_a), axis=1))
    cs = cs3[:, :hp] + cs3[:, hp:2 * hp] + cs3[:, 2 * hp:]
    cs_t = cs.T
    cs_e = _dot(jnp.concatenate(_split3(cs), axis=1), expand_ref[...])
    dt_hi = dt.astype(BF16)
    dt_lo = (dt - dt_hi.astype(F32)).astype(BF16)
    dt_e = _dot(jnp.concatenate([dt_hi, dt_lo], axis=1), expand_ref[0:2 * hp, :])
    xdt = xs * dt_e
    xdt_bf = xdt.astype(BF16)
    cs_last_e = cs_e[q - 1:q, :]
    decay_in = jnp.exp2(cs_e)
    chunk_decay = jnp.exp2(cs_last_e)
    x_end = (xdt * jnp.exp2(cs_last_e - cs_e)).astype(BF16)

    row_i = lax.broadcasted_iota(jnp.int32, (q, q), 0)
    col_i = lax.broadcasted_iota(jnp.int32, (q, q), 1)
    causal = row_i >= col_i
    first_of_pair = lax.broadcasted_iota(jnp.int32, (q, 2 * p), 1) < p

    y_groups = []
    for g in range(SSD_GROUPS):
        bg = b_bf[:, g * SSD_STATE:(g + 1) * SSD_STATE]
        cg = c_bf[:, g * SSD_STATE:(g + 1) * SSD_STATE]
        gsl = slice(g * SSD_GROUP_WIDTH, (g + 1) * SSD_GROUP_WIDTH)
        cb = _dot_nt(cg, bg)
        state = state_ref[g]
        y_off = _dot(cg, state.astype(BF16)) * decay_in[:, gsl]
        y_pairs = []
        for pr in range(SSD_HPG // 2):
            h0 = g * SSD_HPG + 2 * pr
            xp = xdt_bf[:, h0 * p:(h0 + 2) * p]
            parts = []
            for h in (h0, h0 + 1):
                seg = jnp.broadcast_to(cs[:, h:h + 1], (q, q)) - jnp.broadcast_to(cs_t[h:h + 1, :], (q, q))
                decay = jnp.exp2(jnp.where(causal, seg, NEG_BIG))
                parts.append(_dot((cb * decay).astype(BF16), xp))
            y_pairs.append(jnp.where(first_of_pair, parts[0], parts[1]))
        y_groups.append(jnp.concatenate(y_pairs, axis=-1) + y_off)
        new_state = _dot_tn(bg, x_end[:, gsl])
        state_ref[g] = state * chunk_decay[:, gsl] + new_state

    y = jnp.concatenate(y_groups, axis=-1) + dexp_ref[...] * xs
    y = y * zs_ref[...]
    outs = []
    for g in range(SSD_GROUPS):
        gsl = slice(g * SSD_GROUP_WIDTH, (g + 1) * SSD_GROUP_WIDTH)
        yg = y[:, gsl]
        ms = jnp.mean(yg * yg, axis=-1, keepdims=True)
        outs.append(yg * lax.rsqrt(ms + RMS_EPS) * normw_ref[:, gsl])
    o_ref[...] = jnp.concatenate(outs, axis=-1)


def _ssd_mixer(zs, xbc, dt_raw, batch, seq, conv_w, conv_b, dt_bias, a_log, d_skip, norm_w):
    q = SSD_CHUNK
    n_c = seq // q
    heads_pad = LANES
    pad_h = heads_pad - SSD_HEADS
    dtb = jnp.pad(dt_bias, (0, pad_h)).reshape(1, heads_pad)
    alog = jnp.pad(a_log, (0, pad_h)).reshape(1, heads_pad)
    d_exp = jnp.repeat(d_skip, SSD_HEADDIM).reshape(1, SSD_WIDTH)
    t = jnp.arange(q)
    tril = (t[:, None] >= t[None, :]).astype(BF16)
    expand = (jnp.arange(heads_pad)[:, None] == (jnp.arange(SSD_WIDTH) // SSD_HEADDIM)[None, :]).astype(BF16)
    expand3 = jnp.tile(expand, (3, 1))
    row_map = lambda b, c: (b * n_c + c, 0)
    tiles_per_chunk = q // SUBLANES
    tail_map = lambda b, c: (jnp.maximum((b * n_c + c) * tiles_per_chunk - 1, 0), 0)
    return pl.pallas_call(
        _ssd_kernel,
        out_shape=jax.ShapeDtypeStruct((batch * seq, SSD_WIDTH), F32),
        grid=(batch, n_c),
        in_specs=[pl.BlockSpec((q, SSD_WIDTH), row_map),
                  pl.BlockSpec((q, SSD_CONV_CH), row_map),
                  pl.BlockSpec((SUBLANES, SSD_CONV_CH), tail_map),
                  pl.BlockSpec((q, heads_pad), row_map)] + [_vmem_spec()] * 8,
        out_specs=pl.BlockSpec((q, SSD_WIDTH), row_map),
        scratch_shapes=[pltpu.VMEM((SSD_GROUPS, SSD_STATE, SSD_GROUP_WIDTH), F32)],
        compiler_params=_params("parallel", "arbitrary"),
        name="ssd_mixer",
    )(zs, xbc, xbc, dt_raw, conv_w, conv_b.reshape(1, SSD_CONV_CH), dtb, alog, d_exp,
      norm_w.reshape(1, SSD_WIDTH), tril, expand3)


def _head_rmsnorm(x, w_tiled, seg_ones, head_dim):
    sq = x * x
    hi = sq.astype(BF16)
    lo = (sq - hi.astype(F32)).astype(BF16)
    width = x.shape[1]
    blk = seg_ones.shape[0]
    parts = []
    for c0 in range(0, width, blk):
        parts.append(_dot(hi[:, c0:c0 + blk], seg_ones) + _dot(lo[:, c0:c0 + blk], seg_ones))
    ss = parts[0] if len(parts) == 1 else jnp.concatenate(parts, axis=-1)
    return x * lax.rsqrt(ss * (1.0 / head_dim) + RMS_EPS) * w_tiled


def _swa_kernel(q_ref, k_ref, v_ref, kprev_ref, vprev_ref, qw_ref, kw_ref, sink_ref, seg_ref, rep_ref, o_ref):
    blk = SWA_WINDOW
    hd = SWA_HEADDIM
    slab = SWA_HPG * hd
    first = pl.program_id(1) == 0

    n_blk = q_ref.shape[0] // blk
    seg_ones = seg_ref[...]
    qn = _head_rmsnorm(q_ref[...], qw_ref[...], seg_ones, hd) * (hd ** -0.5)
    k_both = jnp.concatenate([kprev_ref[...], k_ref[...]], axis=0)
    k_cat = _head_rmsnorm(k_both, kw_ref[...], seg_ones, hd).astype(BF16)
    v_cat = jnp.concatenate([vprev_ref[...], v_ref[...]], axis=0).astype(BF16)

    qpos = lax.broadcasted_iota(jnp.int32, (blk, blk), 0)
    kpos = lax.broadcasted_iota(jnp.int32, (blk, blk), 1)
    in_prev = kpos > qpos
    prev_bias = jnp.where(first, NEG_BIG, 0.0)
    lane = lax.broadcasted_iota(jnp.int32, (blk, slab), 1)
    head_masks = [(lane >= r * hd) & (lane < (r + 1) * hd) for r in range(SWA_HPG)]

    scores, v_reps = {}, []
    for g in range(SWA_KV_HEADS):
        rep = rep_ref[g]
        k_rep = _dot(k_cat, rep).astype(BF16)
        v_reps.append(_dot(v_cat, rep).astype(BF16))
        for j in range(n_blk):
            q_slab = qn[j * blk:(j + 1) * blk, g * slab:(g + 1) * slab]
            q_stack = jnp.concatenate([jnp.where(mk, q_slab, 0.0) for mk in head_masks], axis=0).astype(BF16)
            scores[g, j] = _dot_nt(q_stack, k_rep[j * blk:(j + 2) * blk])

    probs = {}
    for g in range(SWA_KV_HEADS):
        for j in range(n_blk):
            p_heads = []
            for r in range(SWA_HPG):
                s = scores[g, j][r * blk:(r + 1) * blk, :]
                s_prev = s[:, :blk] + prev_bias if j == 0 else s[:, :blk]
                s_fold = jnp.where(in_prev, s_prev, s[:, blk:])
                sink = sink_ref[g * SWA_HPG + r]
                m = jnp.maximum(jnp.max(s_fold, axis=-1, keepdims=True), sink)
                p = jnp.exp(s_fold - m)
                denom = jnp.sum(p, axis=-1, keepdims=True) + jnp.exp(sink - m)
                pn = p * (1.0 / denom)
                p_heads.append(jnp.concatenate([jnp.where(in_prev, pn, 0.0).astype(BF16),
                                                jnp.where(in_prev, 0.0, pn).astype(BF16)], axis=1))
            probs[g, j] = jnp.concatenate(p_heads, axis=0)

    for g in range(SWA_KV_HEADS):
        for j in range(n_blk):
            o_stack = _dot(probs[g, j], v_reps[g][j * blk:(j + 2) * blk])
            o_slab = jnp.where(head_masks[0], o_stack[0:blk], 0.0)
            for r in range(1, SWA_HPG):
                o_slab = o_slab + jnp.where(head_masks[r], o_stack[r * blk:(r + 1) * blk], 0.0)
            o_ref[j * blk:(j + 1) * blk, g * slab:(g + 1) * slab] = o_slab


def _swa_mixer(q, k, v, batch, seq, q_norm_w, k_norm_w, sinks):
    blk = SWA_WINDOW
    n_blk = SWA_BLOCKS_PER_STEP
    tq = n_blk * blk
    n_b = seq // tq
    qw = jnp.tile(q_norm_w, SWA_HEADS).reshape(1, SWA_WIDTH)
    kw = jnp.tile(k_norm_w, SWA_KV_HEADS).reshape(1, SWA_KV_WIDTH)
    seg = SWA_KV_WIDTH
    lane = jnp.arange(seg)
    seg_ones = (lane[:, None] // SWA_HEADDIM == lane[None, :] // SWA_HEADDIM).astype(BF16)
    rep = (lane[None, :, None] == (jnp.arange(SWA_KV_HEADS)[:, None, None] * SWA_HEADDIM
                                   + lane[None, None, :] % SWA_HEADDIM)).astype(BF16)
    row_map = lambda b, i: (b * n_b + i, 0)
    prev_map = lambda b, i: (jnp.maximum((b * n_b + i) * n_blk - 1, 0), 0)
    return pl.pallas_call(
        _swa_kernel,
        out_shape=jax.ShapeDtypeStruct((batch * seq, SWA_WIDTH), F32),
        grid=(batch, n_b),
        in_specs=[pl.BlockSpec((tq, SWA_WIDTH), row_map),
                  pl.BlockSpec((tq, SWA_KV_WIDTH), row_map),
                  pl.BlockSpec((tq, SWA_KV_WIDTH), row_map),
                  pl.BlockSpec((blk, SWA_KV_WIDTH), prev_map),
                  pl.BlockSpec((blk, SWA_KV_WIDTH), prev_map),
                  _vmem_spec(), _vmem_spec(),
                  pl.BlockSpec(memory_space=pltpu.SMEM),
                  _vmem_spec(), _vmem_spec()],
        out_specs=pl.BlockSpec((tq, SWA_WIDTH), row_map),
        compiler_params=_params("parallel", "parallel"),
        name="swa_mixer",
    )(q, k, v, k, v, qw, kw, sinks, seg_ones, rep)


def _gla_kernel(q_ref, k_ref, v_ref, gate_ref, glr_ref, w2_ref, gb_ref, normw_ref, tril_ref,
                o_ref, state_ref):
    c = GLA_CHUNK
    dk, dv = GLA_HEAD_K, GLA_HEAD_V
    rows = q_ref.shape[0]

    @pl.when(pl.program_id(1) == 0)
    def _():
        state_ref[...] = jnp.zeros_like(state_ref)

    w2 = w2_ref[...]
    g_hi, g_mid, g_lo = _split3(glr_ref[...])
    w_hi, w_mid, w_lo = _split3(w2)
    pre = (_dot(g_hi, w_hi) + _dot(g_hi, w_mid) + _dot(g_mid, w_hi)
           + _dot(g_hi, w_lo) + _dot(g_mid, w_mid) + _dot(g_lo, w_hi)) + gb_ref[...]
    gk = -_softplus(-pre) * (LOG2E / GLA_GATE_NORMALIZER)
    gk3 = _dot(tril_ref[...], jnp.concatenate(_split3(gk), axis=1))
    kw = gk.shape[1]
    bc = gk3[:, :kw] + gk3[:, kw:2 * kw] + gk3[:, 2 * kw:]

    n_chunks = rows // c
    pair = 2 * c
    assert rows % pair == 0
    b_lasts = [bc[(i + 1) * c - 1:(i + 1) * c, :] for i in range(n_chunks)]
    b_last = jnp.concatenate([jnp.broadcast_to(b, (c, kw)) for b in b_lasts], axis=0)
    k_all = k_ref[...]
    q_t = (q_ref[...] * (dk ** -0.5) * jnp.exp2(bc)).astype(BF16)
    k_t = (k_all * jnp.exp2(-bc)).astype(BF16)
    k_end = (k_all * jnp.exp2(b_last - bc)).astype(BF16)
    decs = [jnp.exp2(b) for b in b_lasts]
    v_bf = v_ref[...].astype(BF16)
    row_i = lax.broadcasted_iota(jnp.int32, (pair, pair), 0)
    col_i = lax.broadcasted_iota(jnp.int32, (pair, pair), 1)
    same_chunk_causal = (row_i >= col_i) & (col_i >= (row_i // c) * c)

    heads = []
    for h in range(GLA_HEADS):
        ksl = slice(h * dk, (h + 1) * dk)
        vsl = slice(h * dv, (h + 1) * dv)
        o_intra = []
        for r0 in range(0, rows, pair):
            rsl = slice(r0, r0 + pair)
            att = jnp.where(same_chunk_causal, _dot_nt(q_t[rsl, ksl], k_t[rsl, ksl]), 0.0).astype(BF16)
            o_intra.append(_dot(att, v_bf[rsl, vsl]))
        state = state_ref[h]
        o_inter = []
        for i in range(n_chunks):
            rsl = slice(i * c, (i + 1) * c)
            o_inter.append(_dot_nt(q_t[rsl, ksl], state.astype(BF16)))
            state = state * decs[i][:, ksl] + _dot_tn(v_bf[rsl, vsl], k_end[rsl, ksl])
        state_ref[h] = state
        o = jnp.concatenate(o_intra, axis=0) + jnp.concatenate(o_inter, axis=0)
        ms = jnp.mean(o * o, axis=-1, keepdims=True)
        heads.append(o * lax.rsqrt(ms + RMS_EPS) * normw_ref[...])
    o_ref[...] = jnp.concatenate(heads, axis=-1) * gate_ref[...]


def _gla_mixer(q, k, v, gate, g_lr, batch, seq, gate_w2, gate_b, norm_w):
    tq = GLA_TIME_BLOCK
    n_t = seq // tq
    rank_pad = LANES
    w2 = jnp.pad(gate_w2, ((0, rank_pad - GLA_GATE_RANK), (0, 0)))
    idx = jnp.arange(tq)
    tril = ((idx[:, None] >= idx[None, :]) & (idx[:, None] // GLA_CHUNK == idx[None, :] // GLA_CHUNK)).astype(BF16)
    row_map = lambda b, i: (b * n_t + i, 0)
    return pl.pallas_call(
        _gla_kernel,
        out_shape=jax.ShapeDtypeStruct((batch * seq, GLA_VALUE_WIDTH), F32),
        grid=(batch, n_t),
        in_specs=[pl.BlockSpec((tq, GLA_KEY_WIDTH), row_map),
                  pl.BlockSpec((tq, GLA_KEY_WIDTH), row_map),
                  pl.BlockSpec((tq, GLA_VALUE_WIDTH), row_map),
                  pl.BlockSpec((tq, GLA_VALUE_WIDTH), row_map),
                  pl.BlockSpec((tq, rank_pad), row_map)] + [_vmem_spec()] * 4,
        out_specs=pl.BlockSpec((tq, GLA_VALUE_WIDTH), row_map),
        scratch_shapes=[pltpu.VMEM((GLA_HEADS, GLA_HEAD_V, GLA_HEAD_K), F32)],
        compiler_params=_params("parallel", "arbitrary"),
        name="gla_mixer",
    )(q, k, v, gate, g_lr, w2, gate_b.reshape(1, GLA_KEY_WIDTH), norm_w.reshape(1, GLA_HEAD_V), tril)


def _pad_cols(w, total):
    return jnp.pad(w, ((0, 0), (0, total - w.shape[1])))


def _layer0(h2d, batch, seq, norm_mix, w_in, s5_A_re, s5_A_im, s5_log_dt, s5_B_re, s5_B_im, s5_C_re,
            s5_C_im, s5_D, s5_glu_w, s5_glu_b, ssd_conv_w, ssd_conv_b, ssd_dt_bias, ssd_A_log,
            ssd_D, ssd_norm_w, w_out, norm_ffn, w1, w3, w2):
    z0 = S5_WIDTH
    x0 = z0 + SSD_WIDTH
    dt0 = x0 + SSD_CONV_CH
    pieces = [(0, S5_WIDTH, PLAIN), (z0, SSD_WIDTH, SILU), (x0, SSD_CONV_CH, PLAIN), (dt0, LANES, PLAIN)]
    w_in_p = _pad_cols(w_in, dt0 + LANES).astype(BF16)
    u, zs, xbc, dt_raw = _norm_proj(h2d, norm_mix, w_in_p, pieces, 0, batch, seq)

    ab_re, ab_im, bc_re, bc_im = _s5_discretize(s5_A_re, s5_A_im, s5_log_dt, s5_B_re, s5_B_im)
    wb_re = _block_diag_halves(bc_re).astype(BF16)
    wb_im = _block_diag_halves(bc_im).astype(BF16)
    wc_re = _block_diag_halves(jnp.transpose(s5_C_re, (0, 2, 1))).astype(BF16)
    wc_im = _block_diag_halves(jnp.transpose(s5_C_im, (0, 2, 1))).astype(BF16)
    ya = _s5_mixer(u.reshape(seq * batch, S5_WIDTH), batch, seq, wb_re, wb_im,
                   ab_re.reshape(1, -1), ab_im.reshape(1, -1), wc_re, wc_im,
                   s5_D.reshape(1, S5_WIDTH), s5_glu_w.astype(BF16), s5_glu_b.reshape(1, S5_WIDTH))
    ya = ya.reshape(seq, batch * S5_WIDTH)

    yb = _ssd_mixer(zs, xbc, dt_raw, batch, seq, ssd_conv_w, ssd_conv_b, ssd_dt_bias, ssd_A_log,
                    ssd_D, ssd_norm_w)
    h2d = _out_proj([ya, yb], [True, False], h2d, w_out.astype(BF16), batch, seq)
    return _ffn(h2d, norm_ffn, w1.astype(BF16), w3.astype(BF16), w2.astype(BF16))


def _layer1(h2d, batch, seq, norm_mix, w_in, swa_q_norm, swa_k_norm, swa_sinks, gla_gate_w2,
            gla_gate_b, gla_norm_w, w_out, norm_ffn, w1, w3, w2):
    widths = (SWA_WIDTH, SWA_KV_WIDTH, SWA_KV_WIDTH, GLA_KEY_WIDTH, GLA_KEY_WIDTH,
              GLA_VALUE_WIDTH, GLA_VALUE_WIDTH, LANES)
    starts = np.concatenate([[0], np.cumsum(widths)[:-1]]).tolist()
    kinds = (PLAIN,) * 6 + (SILU, PLAIN)
    pieces = list(zip(starts, widths, kinds))
    w_in_p = _pad_cols(w_in, starts[-1] + LANES).astype(BF16)
    qc, kc, vc, qd, kd, vd, gate, g_lr = _norm_proj(h2d, norm_mix, w_in_p, pieces, None, batch, seq)
    yc = _swa_mixer(qc, kc, vc, batch, seq, swa_q_norm, swa_k_norm, swa_sinks)
    yd = _gla_mixer(qd, kd, vd, gate, g_lr, batch, seq, gla_gate_w2, gla_gate_b, gla_norm_w)
    h2d = _out_proj([yc, yd], [False, False], h2d, w_out.astype(BF16), batch, seq)
    return _ffn(h2d, norm_ffn, w1.astype(BF16), w3.astype(BF16), w2.astype(BF16))


def kernel(x, norm0_mix, w_in0, s5_A_re, s5_A_im, s5_log_dt, s5_B_re, s5_B_im, s5_C_re, s5_C_im, s5_D, s5_glu_w, s5_glu_b, ssd_conv_w, ssd_conv_b, ssd_dt_bias, ssd_A_log, ssd_D, ssd_norm_w, w_out0, norm0_ffn, ffn0_w1, ffn0_w3, ffn0_w2, norm1_mix, w_in1, swa_q_norm, swa_k_norm, swa_sinks, gla_gate_w2, gla_gate_b, gla_norm_w, w_out1, norm1_ffn, ffn1_w1, ffn1_w3, ffn1_w2):
    batch, seq, d = x.shape
    assert batch == SUBLANES and d == D_MODEL and seq % OUT_ROW_BLOCK == 0
    h2d = x.reshape(batch * seq, d)
    h2d = _layer0(h2d, batch, seq, norm0_mix, w_in0, s5_A_re, s5_A_im, s5_log_dt, s5_B_re, s5_B_im,
                  s5_C_re, s5_C_im, s5_D, s5_glu_w, s5_glu_b, ssd_conv_w, ssd_conv_b, ssd_dt_bias,
                  ssd_A_log, ssd_D, ssd_norm_w, w_out0, norm0_ffn, ffn0_w1, ffn0_w3, ffn0_w2)
    h2d = _layer1(h2d, batch, seq, norm1_mix, w_in1, swa_q_norm, swa_k_norm, swa_sinks, gla_gate_w2,
                  gla_gate_b, gla_norm_w, w_out1, norm1_ffn, ffn1_w1, ffn1_w3, ffn1_w2)
    return h2d.reshape(batch, seq, d)
```

```python
import functools

import numpy as np
import jax
import jax.numpy as jnp
from jax import lax
from jax.experimental import pallas as pl
from jax.experimental.pallas import tpu as pltpu

F32 = jnp.float32
BF16 = jnp.bfloat16

LANES = 128
SUBLANES = 8
VMEM_LIMIT_BYTES = 56 * 1024 * 1024

D_MODEL = 2048
RMS_EPS = 1e-6
MIX_WIDTH = D_MODEL

S5_WIDTH = D_MODEL // 4
S5_GROUP_CH = 16
S5_GROUPS = S5_WIDTH // S5_GROUP_CH
S5_STATE = 64
S5_HALF_CH = S5_WIDTH // 2
S5_HALF_GROUPS = S5_GROUPS // 2
S5_HALF_STATES = S5_HALF_GROUPS * S5_STATE
S5_TIME_BLOCK = 128

SSD_WIDTH = MIX_WIDTH - S5_WIDTH
SSD_HEADDIM = 64
SSD_HEADS = SSD_WIDTH // SSD_HEADDIM
SSD_GROUPS = 4
SSD_HPG = SSD_HEADS // SSD_GROUPS
SSD_STATE = 128
SSD_CONV = 4
SSD_CHUNK = 128
SSD_BC_WIDTH = SSD_GROUPS * SSD_STATE
SSD_CONV_CH = SSD_WIDTH + 2 * SSD_BC_WIDTH
SSD_GROUP_WIDTH = SSD_WIDTH // SSD_GROUPS

SWA_HEADDIM = 64
SWA_WIDTH = MIX_WIDTH // 2
SWA_HEADS = SWA_WIDTH // SWA_HEADDIM
SWA_KV_HEADS = 4
SWA_HPG = SWA_HEADS // SWA_KV_HEADS
SWA_WINDOW = 128
SWA_KV_WIDTH = SWA_KV_HEADS * SWA_HEADDIM
SWA_BLOCKS_PER_STEP = 4

GLA_VALUE_WIDTH = MIX_WIDTH - SWA_WIDTH
GLA_KEY_WIDTH = GLA_VALUE_WIDTH // 2
GLA_HEADS = 4
GLA_HEAD_K = GLA_KEY_WIDTH // GLA_HEADS
GLA_HEAD_V = GLA_VALUE_WIDTH // GLA_HEADS
GLA_GATE_RANK = 16
GLA_GATE_NORMALIZER = 16.0
GLA_CHUNK = 64
GLA_TIME_BLOCK = 256

FFN_HIDDEN = -(-(8 * D_MODEL) // (3 * 256)) * 256

PROJ_ROW_BLOCK = 512
FFN_ROW_BLOCK = 1024
FFN_HIDDEN_BLOCK = 512
OUT_ROW_BLOCK = 512
PROJ_COL_CHUNK = 512

NEG_BIG = -1e30


def _params(*semantics):
    return pltpu.CompilerParams(dimension_semantics=semantics, vmem_limit_bytes=VMEM_LIMIT_BYTES)


def _vmem_spec():
    return pl.BlockSpec(memory_space=pltpu.VMEM)


def _dot(a, b):
    return jnp.dot(a, b, preferred_element_type=F32)


def _dot_nt(a, b):
    return lax.dot_general(a, b, (((1,), (1,)), ((), ())), preferred_element_type=F32)


def _dot_tn(a, b):
    return lax.dot_general(a, b, (((0,), (0,)), ((), ())), preferred_element_type=F32)


def _split3(v):
    hi = v.astype(BF16)
    r1 = v - hi.astype(F32)
    mid = r1.astype(BF16)
    lo = (r1 - mid.astype(F32)).astype(BF16)
    return hi, mid, lo


def _dot_exact_rhs01(v, m01):
    hi, mid, lo = _split3(v)
    return _dot(hi, m01) + _dot(mid, m01) + _dot(lo, m01)


def _dot_exact_lhs01(m01, v):
    hi, mid, lo = _split3(v)
    return _dot(m01, hi) + _dot(m01, mid) + _dot(m01, lo)


LOG2E = float(np.log2(np.e))


def _sigmoid(x):
    return 1.0 / (1.0 + jnp.exp2(x * (-LOG2E)))


def _silu(x):
    return x * _sigmoid(x)


def _softplus(x):
    return jnp.maximum(x, 0.0) + jnp.log1p(jnp.exp(-jnp.abs(x)))


def _gelu_tanh(x):
    c = np.sqrt(2.0 / np.pi).astype(np.float32)
    return 0.5 * x * (1.0 + jnp.tanh(c * (x + 0.044715 * (x * x * x))))


def _rmsnorm_rows(x, w):
    ms = jnp.mean(x * x, axis=-1, keepdims=True)
    return x * lax.rsqrt(ms + RMS_EPS) * w


PLAIN, SILU = "plain", "silu"


def _norm_proj_kernel(pieces, x_ref, g_ref, w_ref, *rest):
    out_refs = rest[:len(pieces)]
    xn_ref = rest[len(pieces)]
    xn_ref[...] = _rmsnorm_rows(x_ref[...], g_ref[...]).astype(BF16)
    for (start, width, kind), o_ref in zip(pieces, out_refs):
        for c0 in range(0, width, PROJ_COL_CHUNK):
            cw = min(PROJ_COL_CHUNK, width - c0)
            val = _dot(xn_ref[...], w_ref[:, start + c0:start + c0 + cw])
            o_ref[:, c0:c0 + cw] = _silu(val) if kind == SILU else val


def _norm_proj(x2d, g, w_bf16, pieces, time_major_piece, batch, seq):
    m, d = x2d.shape
    tm = PROJ_ROW_BLOCK
    n_l = seq // tm
    out_shapes, out_specs = [], []
    for idx, (_, width, _) in enumerate(pieces):
        if idx == time_major_piece:
            out_shapes.append(jax.ShapeDtypeStruct((seq, batch * width), F32))
            out_specs.append(pl.BlockSpec((tm, width), lambda i: (i % n_l, i // n_l)))
        else:
            out_shapes.append(jax.ShapeDtypeStruct((m, width), F32))
            out_specs.append(pl.BlockSpec((tm, width), lambda i: (i, 0)))
    return pl.pallas_call(
        functools.partial(_norm_proj_kernel, tuple(pieces)),
        out_shape=tuple(out_shapes),
        grid=(m // tm,),
        in_specs=[pl.BlockSpec((tm, d), lambda i: (i, 0)),
                  pl.BlockSpec((1, d), lambda i: (0, 0)),
                  _vmem_spec()],
        out_specs=tuple(out_specs),
        scratch_shapes=[pltpu.VMEM((tm, d), BF16)],
        compiler_params=_params("parallel"),
        name="norm_proj",
    )(x2d, g.reshape(1, d), w_bf16)


def _out_proj_kernel(splits, *refs):
    n = len(splits)
    y_refs, h_ref, w_ref, o_ref = refs[:n], refs[n], refs[n + 1], refs[n + 2]
    acc = h_ref[...]
    for (start, width), y_ref in zip(splits, y_refs):
        acc = acc + _dot(y_ref[...].astype(BF16), w_ref[start:start + width, :])
    o_ref[...] = acc


def _out_proj(ys, time_major, h2d, w_bf16, batch, seq):
    m, d = h2d.shape
    tm = OUT_ROW_BLOCK
    n_l = seq // tm
    splits, in_specs, start = [], [], 0
    for y, tmaj in zip(ys, time_major):
        width = y.shape[1] // batch if tmaj else y.shape[1]
        splits.append((start, width))
        start += width
        if tmaj:
            in_specs.append(pl.BlockSpec((tm, width), lambda i: (i % n_l, i // n_l)))
        else:
            in_specs.append(pl.BlockSpec((tm, width), lambda i: (i, 0)))
    in_specs += [pl.BlockSpec((tm, d), lambda i: (i, 0)), _vmem_spec()]
    return pl.pallas_call(
        functools.partial(_out_proj_kernel, tuple(splits)),
        out_shape=jax.ShapeDtypeStruct((m, d), F32),
        grid=(m // tm,),
        in_specs=in_specs,
        out_specs=pl.BlockSpec((tm, d), lambda i: (i, 0)),
        compiler_params=_params("parallel"),
        name="out_proj",
    )(*ys, h2d, w_bf16)


def _ffn_kernel(h_ref, g_ref, w1_ref, w3_ref, w2_ref, o_ref, xn_ref):
    j = pl.program_id(1)

    def hidden_block():
        xn = xn_ref[...]
        a = _dot(xn, w1_ref[...])
        b = _dot(xn, w3_ref[...])
        return _dot((_silu(a) * b).astype(BF16), w2_ref[...])

    @pl.when(j == 0)
    def _():
        xn_ref[...] = _rmsnorm_rows(h_ref[...], g_ref[...]).astype(BF16)
        o_ref[...] = h_ref[...] + hidden_block()

    @pl.when(j > 0)
    def _():
        o_ref[...] += hidden_block()


def _ffn(h2d, g, w1, w3, w2):
    m, d = h2d.shape
    hidden = w1.shape[1]
    tm, th = FFN_ROW_BLOCK, FFN_HIDDEN_BLOCK
    return pl.pallas_call(
        _ffn_kernel,
        out_shape=jax.ShapeDtypeStruct((m, d), F32),
        grid=(m // tm, hidden // th),
        in_specs=[pl.BlockSpec((tm, d), lambda i, j: (i, 0)),
                  pl.BlockSpec((1, d), lambda i, j: (0, 0)),
                  pl.BlockSpec((d, th), lambda i, j: (0, j)),
                  pl.BlockSpec((d, th), lambda i, j: (0, j)),
                  pl.BlockSpec((th, d), lambda i, j: (j, 0))],
        out_specs=pl.BlockSpec((tm, d), lambda i, j: (i, 0)),
        scratch_shapes=[pltpu.VMEM((tm, d), BF16)],
        compiler_params=_params("parallel", "arbitrary"),
        name="ffn",
    )(h2d, g.reshape(1, d), w1, w3, w2)


def _s5_discretize_kernel(are_ref, aim_ref, logdt_ref, bre_ref, bim_ref,
                          abre_ref, abim_ref, bcre_ref, bcim_ref):
    a_re, a_im = are_ref[...], aim_ref[...]
    dt = jnp.exp(logdt_ref[...])
    mag = jnp.exp(a_re * dt)
    ab_re = mag * jnp.cos(a_im * dt)
    ab_im = mag * jnp.sin(a_im * dt)
    e_re, e_im = ab_re - 1.0, ab_im
    den = a_re * a_re + a_im * a_im
    coef_re = (e_re * a_re + e_im * a_im) / den
    coef_im = (e_im * a_re - e_re * a_im) / den
    abre_ref[...] = ab_re
    abim_ref[...] = ab_im
    b_re, b_im = bre_ref[...], bim_ref[...]
    c_re, c_im = coef_re[:, None, :], coef_im[:, None, :]
    bcre_ref[...] = c_re * b_re - c_im * b_im
    bcim_ref[...] = c_re * b_im + c_im * b_re


def _s5_discretize(a_re, a_im, log_dt, b_re, b_im):
    g, n = a_re.shape
    c = b_re.shape[2]
    b_re_t = jnp.transpose(b_re, (0, 2, 1))
    b_im_t = jnp.transpose(b_im, (0, 2, 1))
    return pl.pallas_call(
        _s5_discretize_kernel,
        out_shape=(jax.ShapeDtypeStruct((g, n), F32), jax.ShapeDtypeStruct((g, n), F32),
                   jax.ShapeDtypeStruct((g, c, n), F32), jax.ShapeDtypeStruct((g, c, n), F32)),
        name="s5_discretize",
    )(a_re, a_im, log_dt.reshape(g, 1), b_re_t, b_im_t)


def _block_diag_halves(blocks):
    g, r, c = blocks.shape
    hg = g // 2
    eye = jnp.eye(hg, dtype=blocks.dtype)
    b = blocks.reshape(2, hg, r, c)
    return jnp.einsum('hgrc,gk->hgrkc', b, eye).reshape(2, hg * r, hg * c)


def _s5_kernel(u_ref, wbre_ref, wbim_ref, abre_ref, abim_ref, wcre_ref, wcim_ref,
               d_ref, gw_ref, gb_ref, o_ref, xre_ref, xim_ref, hre_ref, him_ref):
    @pl.when(pl.program_id(0) == 0)
    def _():
        hre_ref[...] = jnp.zeros_like(hre_ref)
        him_ref[...] = jnp.zeros_like(him_ref)

    rows = u_ref.shape[0]
    n_steps = rows // SUBLANES
    u = u_ref[...]
    u_bf = u.astype(BF16)
    hs = S5_HALF_STATES
    y_halves = []
    for half in range(2):
        uh = u_bf[:, half * S5_HALF_CH:(half + 1) * S5_HALF_CH]
        xre_ref[...] = _dot(uh, wbre_ref[half])
        xim_ref[...] = _dot(uh, wbim_ref[half])
        a_re = jnp.broadcast_to(abre_ref[:, half * hs:(half + 1) * hs], (SUBLANES, hs))
        a_im = jnp.broadcast_to(abim_ref[:, half * hs:(half + 1) * hs], (SUBLANES, hs))

        def step(t, carry):
            h_re, h_im = carry
            r0 = pl.multiple_of(t * SUBLANES, SUBLANES)
            n_re = a_re * h_re - a_im * h_im + xre_ref[pl.ds(r0, SUBLANES), :]
            n_im = a_re * h_im + a_im * h_re + xim_ref[pl.ds(r0, SUBLANES), :]
            xre_ref[pl.ds(r0, SUBLANES), :] = n_re
            xim_ref[pl.ds(r0, SUBLANES), :] = n_im
            return n_re, n_im

        h0 = (hre_ref[:, half * hs:(half + 1) * hs], him_ref[:, half * hs:(half + 1) * hs])
        h_re, h_im = lax.fori_loop(0, n_steps, step, h0, unroll=4)
        hre_ref[:, half * hs:(half + 1) * hs] = h_re
        him_ref[:, half * hs:(half + 1) * hs] = h_im
        y_halves.append(_dot(xre_ref[...].astype(BF16), wcre_ref[half])
                        - _dot(xim_ref[...].astype(BF16), wcim_ref[half]))
    y = jnp.concatenate(y_halves, axis=-1) + d_ref[...] * u
    y = _gelu_tanh(y)
    gate = _sigmoid(_dot(y.astype(BF16), gw_ref[...]) + gb_ref[...])
    o_ref[...] = y * gate


def _s5_mixer(u_tb, batch, seq, wb_re, wb_im, ab_re, ab_im, wc_re, wc_im, d_skip, glu_w, glu_b):
    rows = S5_TIME_BLOCK * batch
    width = S5_WIDTH
    return pl.pallas_call(
        _s5_kernel,
        out_shape=jax.ShapeDtypeStruct((seq * batch, width), F32),
        grid=(seq // S5_TIME_BLOCK,),
        in_specs=[pl.BlockSpec((rows, width), lambda i: (i, 0))] + [_vmem_spec()] * 9,
        out_specs=pl.BlockSpec((rows, width), lambda i: (i, 0)),
        scratch_shapes=[pltpu.VMEM((rows, S5_HALF_STATES), F32),
                        pltpu.VMEM((rows, S5_HALF_STATES), F32),
                        pltpu.VMEM((batch, 2 * S5_HALF_STATES), F32),
                        pltpu.VMEM((batch, 2 * S5_HALF_STATES), F32)],
        compiler_params=_params("arbitrary"),
        name="s5_mixer",
    )(u_tb, wb_re, wb_im, ab_re, ab_im, wc_re, wc_im, d_skip, glu_w, glu_b)


def _ssd_kernel(zs_ref, xbc_ref, tail_ref, dtraw_ref, convw_ref, convb_ref, dtb_ref, alog_ref, dexp_ref,
                normw_ref, tril_ref, expand_ref, o_ref, state_ref):
    q = SSD_CHUNK
    p = SSD_HEADDIM
    first = pl.program_id(1) == 0

    @pl.when(first)
    def _():
        state_ref[...] = jnp.zeros_like(state_ref)

    x = xbc_ref[...]
    tail = tail_ref[...] * jnp.where(first, 0.0, 1.0)
    row8 = lax.broadcasted_iota(jnp.int32, (SUBLANES, SSD_CONV_CH), 0)
    conv = convb_ref[...] + convw_ref[SSD_CONV - 1:SSD_CONV, :] * x
    for k in range(SSD_CONV - 1):
        back = SSD_CONV - 1 - k
        rolled = pltpu.roll(x, back, 0)
        top = jnp.where(row8 < back, pltpu.roll(tail, back, 0), rolled[0:SUBLANES])
        shifted = jnp.concatenate([top, rolled[SUBLANES:]], axis=0)
        conv = conv + convw_ref[k:k + 1, :] * shifted
    act = _silu(conv)
    xs = act[:, :SSD_WIDTH]
    b_bf = act[:, SSD_WIDTH:SSD_WIDTH + SSD_BC_WIDTH].astype(BF16)
    c_bf = act[:, SSD_WIDTH + SSD_BC_WIDTH:].astype(BF16)

    dt = _softplus(dtraw_ref[...] + dtb_ref[...])
    dt_a = dt * (-jnp.exp(alog_ref[...]) * LOG2E)
    hp = dt.shape[1]
    cs3 = _dot(tril_ref[...], jnp.concatenate(_split3(dt_a), axis=1))
    cs = cs3[:, :hp] + cs3[:, hp:2 * hp] + cs3[:, 2 * hp:]
    cs_t = cs.T
    cs_e = _dot(jnp.concatenate(_split3(cs), axis=1), expand_ref[...])
    dt_hi = dt.astype(BF16)
    dt_lo = (dt - dt_hi.astype(F32)).astype(BF16)
    dt_e = _dot(jnp.concatenate([dt_hi, dt_lo], axis=1), expand_ref[0:2 * hp, :])
    xdt = xs * dt_e
    xdt_bf = xdt.astype(BF16)
    cs_last_e = cs_e[q - 1:q, :]
    decay_in = jnp.exp2(cs_e)
    chunk_decay = jnp.exp2(cs_last_e)
    x_end = (xdt * jnp.exp2(cs_last_e - cs_e)).astype(BF16)

    row_i = lax.broadcasted_iota(jnp.int32, (q, q), 0)
    col_i = lax.broadcasted_iota(jnp.int32, (q, q), 1)
    causal = row_i >= col_i
    first_of_pair = lax.broadcasted_iota(jnp.int32, (q, 2 * p), 1) < p

    y_groups = []
    for g in range(SSD_GROUPS):
        bg = b_bf[:, g * SSD_STATE:(g + 1) * SSD_STATE]
        cg = c_bf[:, g * SSD_STATE:(g + 1) * SSD_STATE]
        gsl = slice(g * SSD_GROUP_WIDTH, (g + 1) * SSD_GROUP_WIDTH)
        cb = _dot_nt(cg, bg)
        state = state_ref[g]
        y_off = _dot(cg, state.astype(BF16)) * decay_in[:, gsl]
        y_pairs = []
        for pr in range(SSD_HPG // 2):
            h0 = g * SSD_HPG + 2 * pr
            xp = xdt_bf[:, h0 * p:(h0 + 2) * p]
            parts = []
            for h in (h0, h0 + 1):
                seg = jnp.broadcast_to(cs[:, h:h + 1], (q, q)) - jnp.broadcast_to(cs_t[h:h + 1, :], (q, q))
                decay = jnp.exp2(jnp.where(causal, seg, NEG_BIG))
                parts.append(_dot((cb * decay).astype(BF16), xp))
            y_pairs.append(jnp.where(first_of_pair, parts[0], parts[1]))
        y_groups.append(jnp.concatenate(y_pairs, axis=-1) + y_off)
        new_state = _dot_tn(bg, x_end[:, gsl])
        state_ref[g] = state * chunk_decay[:, gsl] + new_state

    y = jnp.concatenate(y_groups, axis=-1) + dexp_ref[...] * xs
    y = y * zs_ref[...]
    outs = []
    for g in range(SSD_GROUPS):
        gsl = slice(g * SSD_GROUP_WIDTH, (g + 1) * SSD_GROUP_WIDTH)
        yg = y[:, gsl]
        ms = jnp.mean(yg * yg, axis=-1, keepdims=True)
        outs.append(yg * lax.rsqrt(ms + RMS_EPS) * normw_ref[:, gsl])
    o_ref[...] = jnp.concatenate(outs, axis=-1)


def _ssd_mixer(zs, xbc, dt_raw, batch, seq, conv_w, conv_b, dt_bias, a_log, d_skip, norm_w):
    q = SSD_CHUNK
    n_c = seq // q
    heads_pad = LANES
    pad_h = heads_pad - SSD_HEADS
    dtb = jnp.pad(dt_bias, (0, pad_h)).reshape(1, heads_pad)
    alog = jnp.pad(a_log, (0, pad_h)).reshape(1, heads_pad)
    d_exp = jnp.repeat(d_skip, SSD_HEADDIM).reshape(1, SSD_WIDTH)
    t = jnp.arange(q)
    tril = (t[:, None] >= t[None, :]).astype(BF16)
    expand = (jnp.arange(heads_pad)[:, None] == (jnp.arange(SSD_WIDTH) // SSD_HEADDIM)[None, :]).astype(BF16)
    expand3 = jnp.tile(expand, (3, 1))
    row_map = lambda b, c: (b * n_c + c, 0)
    tiles_per_chunk = q // SUBLANES
    tail_map = lambda b, c: (jnp.maximum((b * n_c + c) * tiles_per_chunk - 1, 0), 0)
    return pl.pallas_call(
        _ssd_kernel,
        out_shape=jax.ShapeDtypeStruct((batch * seq, SSD_WIDTH), F32),
        grid=(batch, n_c),
        in_specs=[pl.BlockSpec((q, SSD_WIDTH), row_map),
                  pl.BlockSpec((q, SSD_CONV_CH), row_map),
                  pl.BlockSpec((SUBLANES, SSD_CONV_CH), tail_map),
                  pl.BlockSpec((q, heads_pad), row_map)] + [_vmem_spec()] * 8,
        out_specs=pl.BlockSpec((q, SSD_WIDTH), row_map),
        scratch_shapes=[pltpu.VMEM((SSD_GROUPS, SSD_STATE, SSD_GROUP_WIDTH), F32)],
        compiler_params=_params("parallel", "arbitrary"),
        name="ssd_mixer",
    )(zs, xbc, xbc, dt_raw, conv_w, conv_b.reshape(1, SSD_CONV_CH), dtb, alog, d_exp,
      norm_w.reshape(1, SSD_WIDTH), tril, expand3)


def _head_rmsnorm(x, w_tiled, seg_ones, head_dim):
    sq = x * x
    hi = sq.astype(BF16)
    lo = (sq - hi.astype(F32)).astype(BF16)
    width = x.shape[1]
    blk = seg_ones.shape[0]
    parts = []
    for c0 in range(0, width, blk):
        parts.append(_dot(hi[:, c0:c0 + blk], seg_ones) + _dot(lo[:, c0:c0 + blk], seg_ones))
    ss = parts[0] if len(parts) == 1 else jnp.concatenate(parts, axis=-1)
    return x * lax.rsqrt(ss * (1.0 / head_dim) + RMS_EPS) * w_tiled


def _swa_kernel(q_ref, k_ref, v_ref, kprev_ref, vprev_ref, qw_ref, kw_ref, sink_ref, seg_ref, rep_ref, o_ref):
    blk = SWA_WINDOW
    hd = SWA_HEADDIM
    slab = SWA_HPG * hd
    first = pl.program_id(1) == 0

    n_blk = q_ref.shape[0] // blk
    seg_ones = seg_ref[...]
    qn = _head_rmsnorm(q_ref[...], qw_ref[...], seg_ones, hd) * (hd ** -0.5)
    k_both = jnp.concatenate([kprev_ref[...], k_ref[...]], axis=0)
    k_cat = _head_rmsnorm(k_both, kw_ref[...], seg_ones, hd).astype(BF16)
    v_cat = jnp.concatenate([vprev_ref[...], v_ref[...]], axis=0).astype(BF16)

    qpos = lax.broadcasted_iota(jnp.int32, (blk, blk), 0)
    kpos = lax.broadcasted_iota(jnp.int32, (blk, blk), 1)
    in_prev = kpos > qpos
    prev_bias = jnp.where(first, NEG_BIG, 0.0)
    lane = lax.broadcasted_iota(jnp.int32, (blk, slab), 1)
    head_masks = [(lane >= r * hd) & (lane < (r + 1) * hd) for r in range(SWA_HPG)]

    scores, v_reps = {}, []
    for g in range(SWA_KV_HEADS):
        rep = rep_ref[g]
        k_rep = _dot(k_cat, rep).astype(BF16)
        v_reps.append(_dot(v_cat, rep).astype(BF16))
        for j in range(n_blk):
            q_slab = qn[j * blk:(j + 1) * blk, g * slab:(g + 1) * slab]
            q_stack = jnp.concatenate([jnp.where(mk, q_slab, 0.0) for mk in head_masks], axis=0).astype(BF16)
            scores[g, j] = _dot_nt(q_stack, k_rep[j * blk:(j + 2) * blk])

    probs = {}
    for g in range(SWA_KV_HEADS):
        for j in range(n_blk):
            p_heads = []
            for r in range(SWA_HPG):
                s = scores[g, j][r * blk:(r + 1) * blk, :]
                s_prev = s[:, :blk] + prev_bias if j == 0 else s[:, :blk]
                s_fold = jnp.where(in_prev, s_prev, s[:, blk:])
                sink = sink_ref[g * SWA_HPG + r]
                m = jnp.maximum(jnp.max(s_fold, axis=-1, keepdims=True), sink)
                p = jnp.exp(s_fold - m)
                denom = jnp.sum(p, axis=-1, keepdims=True) + jnp.exp(sink - m)
                pn = p * (1.0 / denom)
                p_heads.append(jnp.concatenate([jnp.where(in_prev, pn, 0.0).astype(BF16),
                                                jnp.where(in_prev, 0.0, pn).astype(BF16)], axis=1))
            probs[g, j] = jnp.concatenate(p_heads, axis=0)

    for g in range(SWA_KV_HEADS):
        for j in range(n_blk):
            o_stack = _dot(probs[g, j], v_reps[g][j * blk:(j + 2) * blk])
            o_slab = jnp.where(head_masks[0], o_stack[0:blk], 0.0)
            for r in range(1, SWA_HPG):
                o_slab = o_slab + jnp.where(head_masks[r], o_stack[r * blk:(r + 1) * blk], 0.0)
            o_ref[j * blk:(j + 1) * blk, g * slab:(g + 1) * slab] = o_slab


def _swa_mixer(q, k, v, batch, seq, q_norm_w, k_norm_w, sinks):
    blk = SWA_WINDOW
    n_blk = SWA_BLOCKS_PER_STEP
    tq = n_blk * blk
    n_b = seq // tq
    qw = jnp.tile(q_norm_w, SWA_HEADS).reshape(1, SWA_WIDTH)
    kw = jnp.tile(k_norm_w, SWA_KV_HEADS).reshape(1, SWA_KV_WIDTH)
    seg = SWA_KV_WIDTH
    lane = jnp.arange(seg)
    seg_ones = (lane[:, None] // SWA_HEADDIM == lane[None, :] // SWA_HEADDIM).astype(BF16)
    rep = (lane[None, :, None] == (jnp.arange(SWA_KV_HEADS)[:, None, None] * SWA_HEADDIM
                                   + lane[None, None, :] % SWA_HEADDIM)).astype(BF16)
    row_map = lambda b, i: (b * n_b + i, 0)
    prev_map = lambda b, i: (jnp.maximum((b * n_b + i) * n_blk - 1, 0), 0)
    return pl.pallas_call(
        _swa_kernel,
        out_shape=jax.ShapeDtypeStruct((batch * seq, SWA_WIDTH), F32),
        grid=(batch, n_b),
        in_specs=[pl.BlockSpec((tq, SWA_WIDTH), row_map),
                  pl.BlockSpec((tq, SWA_KV_WIDTH), row_map),
                  pl.BlockSpec((tq, SWA_KV_WIDTH), row_map),
                  pl.BlockSpec((blk, SWA_KV_WIDTH), prev_map),
                  pl.BlockSpec((blk, SWA_KV_WIDTH), prev_map),
                  _vmem_spec(), _vmem_spec(),
                  pl.BlockSpec(memory_space=pltpu.SMEM),
                  _vmem_spec(), _vmem_spec()],
        out_specs=pl.BlockSpec((tq, SWA_WIDTH), row_map),
        compiler_params=_params("parallel", "parallel"),
        name="swa_mixer",
    )(q, k, v, k, v, qw, kw, sinks, seg_ones, rep)


def _gla_kernel(q_ref, k_ref, v_ref, gate_ref, glr_ref, w2_ref, gb_ref, normw_ref, tril_ref,
                o_ref, state_ref):
    c = GLA_CHUNK
    dk, dv = GLA_HEAD_K, GLA_HEAD_V
    rows = q_ref.shape[0]

    @pl.when(pl.program_id(1) == 0)
    def _():
        state_ref[...] = jnp.zeros_like(state_ref)

    w2 = w2_ref[...]
    g_hi, g_mid, g_lo = _split3(glr_ref[...])
    w_hi, w_mid, w_lo = _split3(w2)
    pre = (_dot(g_hi, w_hi) + _dot(g_hi, w_mid) + _dot(g_mid, w_hi)
           + _dot(g_hi, w_lo) + _dot(g_mid, w_mid) + _dot(g_lo, w_hi)) + gb_ref[...]
    gk = -_softplus(-pre) * (LOG2E / GLA_GATE_NORMALIZER)
    gk3 = _dot(tril_ref[...], jnp.concatenate(_split3(gk), axis=1))
    kw = gk.shape[1]
    bc = gk3[:, :kw] + gk3[:, kw:2 * kw] + gk3[:, 2 * kw:]

    n_chunks = rows // c
    pair = 2 * c
    assert rows % pair == 0
    b_lasts = [bc[(i + 1) * c - 1:(i + 1) * c, :] for i in range(n_chunks)]
    b_last = jnp.concatenate([jnp.broadcast_to(b, (c, kw)) for b in b_lasts], axis=0)
    k_all = k_ref[...]
    q_t = (q_ref[...] * (dk ** -0.5) * jnp.exp2(bc)).astype(BF16)
    k_t = (k_all * jnp.exp2(-bc)).astype(BF16)
    k_end = (k_all * jnp.exp2(b_last - bc)).astype(BF16)
    decs = [jnp.exp2(b) for b in b_lasts]
    v_bf = v_ref[...].astype(BF16)
    row_i = lax.broadcasted_iota(jnp.int32, (pair, pair), 0)
    col_i = lax.broadcasted_iota(jnp.int32, (pair, pair), 1)
    same_chunk_causal = (row_i >= col_i) & (col_i >= (row_i // c) * c)

    heads = []
    for h in range(GLA_HEADS):
        ksl = slice(h * dk, (h + 1) * dk)
        vsl = slice(h * dv, (h + 1) * dv)
        o_intra = []
        for r0 in range(0, rows, pair):
            rsl = slice(r0, r0 + pair)
            att = jnp.where(same_chunk_causal, _dot_nt(q_t[rsl, ksl], k_t[rsl, ksl]), 0.0).astype(BF16)
            o_intra.append(_dot(att, v_bf[rsl, vsl]))
        state = state_ref[h]
        o_inter = []
        for i in range(n_chunks):
            rsl = slice(i * c, (i + 1) * c)
            o_inter.append(_dot_nt(q_t[rsl, ksl], state.astype(BF16)))
            state = state * decs[i][:, ksl] + _dot_tn(v_bf[rsl, vsl], k_end[rsl, ksl])
        state_ref[h] = state
        o = jnp.concatenate(o_intra, axis=0) + jnp.concatenate(o_inter, axis=0)
        ms = jnp.mean(o * o, axis=-1, keepdims=True)
        heads.append(o * lax.rsqrt(ms + RMS_EPS) * normw_ref[...])
    o_ref[...] = jnp.concatenate(heads, axis=-1) * gate_ref[...]


def _gla_mixer(q, k, v, gate, g_lr, batch, seq, gate_w2, gate_b, norm_w):
    tq = GLA_TIME_BLOCK
    n_t = seq // tq
    rank_pad = LANES
    w2 = jnp.pad(gate_w2, ((0, rank_pad - GLA_GATE_RANK), (0, 0)))
    idx = jnp.arange(tq)
    tril = ((idx[:, None] >= idx[None, :]) & (idx[:, None] // GLA_CHUNK == idx[None, :] // GLA_CHUNK)).astype(BF16)
    row_map = lambda b, i: (b * n_t + i, 0)
    return pl.pallas_call(
        _gla_kernel,
        out_shape=jax.ShapeDtypeStruct((batch * seq, GLA_VALUE_WIDTH), F32),
        grid=(batch, n_t),
        in_specs=[pl.BlockSpec((tq, GLA_KEY_WIDTH), row_map),
                  pl.BlockSpec((tq, GLA_KEY_WIDTH), row_map),
                  pl.BlockSpec((tq, GLA_VALUE_WIDTH), row_map),
                  pl.BlockSpec((tq, GLA_VALUE_WIDTH), row_map),
                  pl.BlockSpec((tq, rank_pad), row_map)] + [_vmem_spec()] * 4,
        out_specs=pl.BlockSpec((tq, GLA_VALUE_WIDTH), row_map),
        scratch_shapes=[pltpu.VMEM((GLA_HEADS, GLA_HEAD_V, GLA_HEAD_K), F32)],
        compiler_params=_params("parallel", "arbitrary"),
        name="gla_mixer",
    )(q, k, v, gate, g_lr, w2, gate_b.reshape(1, GLA_KEY_WIDTH), norm_w.reshape(1, GLA_HEAD_V), tril)


def _pad_cols(w, total):
    return jnp.pad(w, ((0, 0), (0, total - w.shape[1])))


def _layer0(h2d, batch, seq, norm_mix, w_in, s5_A_re, s5_A_im, s5_log_dt, s5_B_re, s5_B_im, s5_C_re,
            s5_C_im, s5_D, s5_glu_w, s5_glu_b, ssd_conv_w, ssd_conv_b, ssd_dt_bias, ssd_A_log,
            ssd_D, ssd_norm_w, w_out, norm_ffn, w1, w3, w2):
    z0 = S5_WIDTH
    x0 = z0 + SSD_WIDTH
    dt0 = x0 + SSD_CONV_CH
    pieces = [(0, S5_WIDTH, PLAIN), (z0, SSD_WIDTH, SILU), (x0, SSD_CONV_CH, PLAIN), (dt0, LANES, PLAIN)]
    w_in_p = _pad_cols(w_in.astype(BF16), dt0 + LANES)
    u, zs, xbc, dt_raw = _norm_proj(h2d, norm_mix, w_in_p, pieces, 0, batch, seq)

    ab_re, ab_im, bc_re, bc_im = _s5_discretize(s5_A_re, s5_A_im, s5_log_dt, s5_B_re, s5_B_im)
    wb_re = _block_diag_halves(bc_re).astype(BF16)
    wb_im = _block_diag_halves(bc_im).astype(BF16)
    wc_re = _block_diag_halves(jnp.transpose(s5_C_re, (0, 2, 1))).astype(BF16)
    wc_im = _block_diag_halves(jnp.transpose(s5_C_im, (0, 2, 1))).astype(BF16)
    ya = _s5_mixer(u.reshape(seq * batch, S5_WIDTH), batch, seq, wb_re, wb_im,
                   ab_re.reshape(1, -1), ab_im.reshape(1, -1), wc_re, wc_im,
                   s5_D.reshape(1, S5_WIDTH), s5_glu_w.astype(BF16), s5_glu_b.reshape(1, S5_WIDTH))
    ya = ya.reshape(seq, batch * S5_WIDTH)

    yb = _ssd_mixer(zs, xbc, dt_raw, batch, seq, ssd_conv_w, ssd_conv_b, ssd_dt_bias, ssd_A_log,
                    ssd_D, ssd_norm_w)
    h2d = _out_proj([ya, yb], [True, False], h2d, w_out.astype(BF16), batch, seq)
    return _ffn(h2d, norm_ffn, w1.astype(BF16), w3.astype(BF16), w2.astype(BF16))


def _layer1(h2d, batch, seq, norm_mix, w_in, swa_q_norm, swa_k_norm, swa_sinks, gla_gate_w2,
            gla_gate_b, gla_norm_w, w_out, norm_ffn, w1, w3, w2):
    widths = (SWA_WIDTH, SWA_KV_WIDTH, SWA_KV_WIDTH, GLA_KEY_WIDTH, GLA_KEY_WIDTH,
              GLA_VALUE_WIDTH, GLA_VALUE_WIDTH, LANES)
    starts = np.concatenate([[0], np.cumsum(widths)[:-1]]).tolist()
    kinds = (PLAIN,) * 6 + (SILU, PLAIN)
    pieces = list(zip(starts, widths, kinds))
    w_in_p = _pad_cols(w_in.astype(BF16), starts[-1] + LANES)
    qc, kc, vc, qd, kd, vd, gate, g_lr = _norm_proj(h2d, norm_mix, w_in_p, pieces, None, batch, seq)
    yc = _swa_mixer(qc, kc, vc, batch, seq, swa_q_norm, swa_k_norm, swa_sinks)
    yd = _gla_mixer(qd, kd, vd, gate, g_lr, batch, seq, gla_gate_w2, gla_gate_b, gla_norm_w)
    h2d = _out_proj([yc, yd], [False, False], h2d, w_out.astype(BF16), batch, seq)
    return _ffn(h2d, norm_ffn, w1.astype(BF16), w3.astype(BF16), w2.astype(BF16))


def kernel(x, norm0_mix, w_in0, s5_A_re, s5_A_im, s5_log_dt, s5_B_re, s5_B_im, s5_C_re, s5_C_im, s5_D, s5_glu_w, s5_glu_b, ssd_conv_w, ssd_conv_b, ssd_dt_bias, ssd_A_log, ssd_D, ssd_norm_w, w_out0, norm0_ffn, ffn0_w1, ffn0_w3, ffn0_w2, norm1_mix, w_in1, swa_q_norm, swa_k_norm, swa_sinks, gla_gate_w2, gla_gate_b, gla_norm_w, w_out1, norm1_ffn, ffn1_w1, ffn1_w3, ffn1_w2):
    batch, seq, d = x.shape
    assert batch == SUBLANES and d == D_MODEL and seq % OUT_ROW_BLOCK == 0
    h2d = x.reshape(batch * seq, d)
    h2d = _layer0(h2d, batch, seq, norm0_mix, w_in0, s5_A_re, s5_A_im, s5_log_dt, s5_B_re, s5_B_im,
                  s5_C_re, s5_C_im, s5_D, s5_glu_w, s5_glu_b, ssd_conv_w, ssd_conv_b, ssd_dt_bias,
                  ssd_A_log, ssd_D, ssd_norm_w, w_out0, norm0_ffn, ffn0_w1, ffn0_w3, ffn0_w2)
    h2d = _layer1(h2d, batch, seq, norm1_mix, w_in1, swa_q_norm, swa_k_norm, swa_sinks, gla_gate_w2,
                  gla_gate_b, gla_norm_w, w_out1, norm1_ffn, ffn1_w1, ffn1_w3, ffn1_w2)
    return h2d.reshape(batch, seq, d)
```

```python
import functools

import numpy as np
import jax
import jax.numpy as jnp
from jax import lax
from jax.experimental import pallas as pl
from jax.experimental.pallas import tpu as pltpu

F32 = jnp.float32
BF16 = jnp.bfloat16

LANES = 128
SUBLANES = 8
VMEM_LIMIT_BYTES = 56 * 1024 * 1024

D_MODEL = 2048
RMS_EPS = 1e-6
MIX_WIDTH = D_MODEL

S5_WIDTH = D_MODEL // 4
S5_GROUP_CH = 16
S5_GROUPS = S5_WIDTH // S5_GROUP_CH
S5_STATE = 64
S5_HALF_CH = S5_WIDTH // 2
S5_HALF_GROUPS = S5_GROUPS // 2
S5_HALF_STATES = S5_HALF_GROUPS * S5_STATE
S5_TIME_BLOCK = 128

SSD_WIDTH = MIX_WIDTH - S5_WIDTH
SSD_HEADDIM = 64
SSD_HEADS = SSD_WIDTH // SSD_HEADDIM
SSD_GROUPS = 4
SSD_HPG = SSD_HEADS // SSD_GROUPS
SSD_STATE = 128
SSD_CONV = 4
SSD_CHUNK = 128
SSD_BC_WIDTH = SSD_GROUPS * SSD_STATE
SSD_CONV_CH = SSD_WIDTH + 2 * SSD_BC_WIDTH
SSD_GROUP_WIDTH = SSD_WIDTH // SSD_GROUPS

SWA_HEADDIM = 64
SWA_WIDTH = MIX_WIDTH // 2
SWA_HEADS = SWA_WIDTH // SWA_HEADDIM
SWA_KV_HEADS = 4
SWA_HPG = SWA_HEADS // SWA_KV_HEADS
SWA_WINDOW = 128
SWA_KV_WIDTH = SWA_KV_HEADS * SWA_HEADDIM
SWA_BLOCKS_PER_STEP = 4

GLA_VALUE_WIDTH = MIX_WIDTH - SWA_WIDTH
GLA_KEY_WIDTH = GLA_VALUE_WIDTH // 2
GLA_HEADS = 4
GLA_HEAD_K = GLA_KEY_WIDTH // GLA_HEADS
GLA_HEAD_V = GLA_VALUE_WIDTH // GLA_HEADS
GLA_GATE_RANK = 16
GLA_GATE_NORMALIZER = 16.0
GLA_CHUNK = 64
GLA_TIME_BLOCK = 256

FFN_HIDDEN = -(-(8 * D_MODEL) // (3 * 256)) * 256

PROJ_ROW_BLOCK = 512
FFN_ROW_BLOCK = 1024
FFN_HIDDEN_BLOCK = 512
OUT_ROW_BLOCK = 512
PROJ_COL_CHUNK = 512

NEG_BIG = -1e30


def _params(*semantics):
    return pltpu.CompilerParams(dimension_semantics=semantics, vmem_limit_bytes=VMEM_LIMIT_BYTES)


def _vmem_spec():
    return pl.BlockSpec(memory_space=pltpu.VMEM)


def _dot(a, b):
    return jnp.dot(a, b, preferred_element_type=F32)


def _dot_nt(a, b):
    return lax.dot_general(a, b, (((1,), (1,)), ((), ())), preferred_element_type=F32)


def _dot_tn(a, b):
    return lax.dot_general(a, b, (((0,), (0,)), ((), ())), preferred_element_type=F32)


def _split3(v):
    hi = v.astype(BF16)
    r1 = v - hi.astype(F32)
    mid = r1.astype(BF16)
    lo = (r1 - mid.astype(F32)).astype(BF16)
    return hi, mid, lo


def _dot_exact_rhs01(v, m01):
    hi, mid, lo = _split3(v)
    return _dot(hi, m01) + _dot(mid, m01) + _dot(lo, m01)


def _dot_exact_lhs01(m01, v):
    hi, mid, lo = _split3(v)
    return _dot(m01, hi) + _dot(m01, mid) + _dot(m01, lo)


LOG2E = float(np.log2(np.e))


def _sigmoid(x):
    return 1.0 / (1.0 + jnp.exp2(x * (-LOG2E)))


def _silu(x):
    return x * _sigmoid(x)


def _softplus(x):
    return jnp.maximum(x, 0.0) + jnp.log1p(jnp.exp(-jnp.abs(x)))


def _gelu_tanh(x):
    c = np.sqrt(2.0 / np.pi).astype(np.float32)
    return 0.5 * x * (1.0 + jnp.tanh(c * (x + 0.044715 * (x * x * x))))


def _rmsnorm_rows(x, w):
    ms = jnp.mean(x * x, axis=-1, keepdims=True)
    return x * lax.rsqrt(ms + RMS_EPS) * w


PLAIN, SILU = "plain", "silu"


def _norm_proj_kernel(pieces, x_ref, g_ref, w_ref, *rest):
    out_refs = rest[:len(pieces)]
    xn_ref = rest[len(pieces)]
    xn_ref[...] = _rmsnorm_rows(x_ref[...], g_ref[...]).astype(BF16)
    for (start, width, kind), o_ref in zip(pieces, out_refs):
        for c0 in range(0, width, PROJ_COL_CHUNK):
            cw = min(PROJ_COL_CHUNK, width - c0)
            val = _dot(xn_ref[...], w_ref[:, start + c0:start + c0 + cw])
            o_ref[:, c0:c0 + cw] = _silu(val) if kind == SILU else val


def _norm_proj(x2d, g, w_bf16, pieces, time_major_piece, batch, seq):
    m, d = x2d.shape
    tm = PROJ_ROW_BLOCK
    n_l = seq // tm
    out_shapes, out_specs = [], []
    for idx, (_, width, _) in enumerate(pieces):
        if idx == time_major_piece:
            out_shapes.append(jax.ShapeDtypeStruct((seq, batch * width), F32))
            out_specs.append(pl.BlockSpec((tm, width), lambda i: (i % n_l, i // n_l)))
        else:
            out_shapes.append(jax.ShapeDtypeStruct((m, width), F32))
            out_specs.append(pl.BlockSpec((tm, width), lambda i: (i, 0)))
    return pl.pallas_call(
        functools.partial(_norm_proj_kernel, tuple(pieces)),
        out_shape=tuple(out_shapes),
        grid=(m // tm,),
        in_specs=[pl.BlockSpec((tm, d), lambda i: (i, 0)),
                  pl.BlockSpec((1, d), lambda i: (0, 0)),
                  _vmem_spec()],
        out_specs=tuple(out_specs),
        scratch_shapes=[pltpu.VMEM((tm, d), BF16)],
        compiler_params=_params("parallel"),
        name="norm_proj",
    )(x2d, g.reshape(1, d), w_bf16)


def _out_proj_kernel(splits, *refs):
    n = len(splits)
    y_refs, h_ref, w_ref, o_ref = refs[:n], refs[n], refs[n + 1], refs[n + 2]
    acc = h_ref[...]
    for (start, width), y_ref in zip(splits, y_refs):
        acc = acc + _dot(y_ref[...], w_ref[start:start + width, :])
    o_ref[...] = acc


def _out_proj(ys, time_major, h2d, w_bf16, batch, seq):
    m, d = h2d.shape
    tm = OUT_ROW_BLOCK
    n_l = seq // tm
    splits, in_specs, start = [], [], 0
    for y, tmaj in zip(ys, time_major):
        width = y.shape[1] // batch if tmaj else y.shape[1]
        splits.append((start, width))
        start += width
        if tmaj:
            in_specs.append(pl.BlockSpec((tm, width), lambda i: (i % n_l, i // n_l)))
        else:
            in_specs.append(pl.BlockSpec((tm, width), lambda i: (i, 0)))
    in_specs += [pl.BlockSpec((tm, d), lambda i: (i, 0)), _vmem_spec()]
    return pl.pallas_call(
        functools.partial(_out_proj_kernel, tuple(splits)),
        out_shape=jax.ShapeDtypeStruct((m, d), F32),
        grid=(m // tm,),
        in_specs=in_specs,
        out_specs=pl.BlockSpec((tm, d), lambda i: (i, 0)),
        compiler_params=_params("parallel"),
        name="out_proj",
    )(*ys, h2d, w_bf16)


def _ffn_kernel(h_ref, g_ref, w1_ref, w3_ref, w2_ref, o_ref, xn_ref):
    j = pl.program_id(1)

    def hidden_block():
        xn = xn_ref[...]
        a = _dot(xn, w1_ref[...])
        b = _dot(xn, w3_ref[...])
        return _dot((_silu(a) * b).astype(BF16), w2_ref[...])

    @pl.when(j == 0)
    def _():
        xn_ref[...] = _rmsnorm_rows(h_ref[...], g_ref[...]).astype(BF16)
        o_ref[...] = h_ref[...] + hidden_block()

    @pl.when(j > 0)
    def _():
        o_ref[...] += hidden_block()


def _ffn(h2d, g, w1, w3, w2):
    m, d = h2d.shape
    hidden = w1.shape[1]
    tm, th = FFN_ROW_BLOCK, FFN_HIDDEN_BLOCK
    return pl.pallas_call(
        _ffn_kernel,
        out_shape=jax.ShapeDtypeStruct((m, d), F32),
        grid=(m // tm, hidden // th),
        in_specs=[pl.BlockSpec((tm, d), lambda i, j: (i, 0)),
                  pl.BlockSpec((1, d), lambda i, j: (0, 0)),
                  pl.BlockSpec((d, th), lambda i, j: (0, j)),
                  pl.BlockSpec((d, th), lambda i, j: (0, j)),
                  pl.BlockSpec((th, d), lambda i, j: (j, 0))],
        out_specs=pl.BlockSpec((tm, d), lambda i, j: (i, 0)),
        scratch_shapes=[pltpu.VMEM((tm, d), BF16)],
        compiler_params=_params("parallel", "arbitrary"),
        name="ffn",
    )(h2d, g.reshape(1, d), w1, w3, w2)


def _s5_discretize_kernel(are_ref, aim_ref, logdt_ref, bre_ref, bim_ref,
                          abre_ref, abim_ref, bcre_ref, bcim_ref):
    a_re, a_im = are_ref[...], aim_ref[...]
    dt = jnp.exp(logdt_ref[...])
    mag = jnp.exp(a_re * dt)
    ab_re = mag * jnp.cos(a_im * dt)
    ab_im = mag * jnp.sin(a_im * dt)
    e_re, e_im = ab_re - 1.0, ab_im
    den = a_re * a_re + a_im * a_im
    coef_re = (e_re * a_re + e_im * a_im) / den
    coef_im = (e_im * a_re - e_re * a_im) / den
    abre_ref[...] = ab_re
    abim_ref[...] = ab_im
    b_re, b_im = bre_ref[...], bim_ref[...]
    c_re, c_im = coef_re[:, None, :], coef_im[:, None, :]
    bcre_ref[...] = c_re * b_re - c_im * b_im
    bcim_ref[...] = c_re * b_im + c_im * b_re


def _s5_discretize(a_re, a_im, log_dt, b_re, b_im):
    g, n = a_re.shape
    c = b_re.shape[2]
    b_re_t = jnp.transpose(b_re, (0, 2, 1))
    b_im_t = jnp.transpose(b_im, (0, 2, 1))
    return pl.pallas_call(
        _s5_discretize_kernel,
        out_shape=(jax.ShapeDtypeStruct((g, n), F32), jax.ShapeDtypeStruct((g, n), F32),
                   jax.ShapeDtypeStruct((g, c, n), F32), jax.ShapeDtypeStruct((g, c, n), F32)),
        name="s5_discretize",
    )(a_re, a_im, log_dt.reshape(g, 1), b_re_t, b_im_t)


def _block_diag_halves(blocks):
    g, r, c = blocks.shape
    hg = g // 2
    eye = jnp.eye(hg, dtype=blocks.dtype)
    b = blocks.reshape(2, hg, r, c)
    return jnp.einsum('hgrc,gk->hgrkc', b, eye).reshape(2, hg * r, hg * c)


def _s5_kernel(u_ref, wbre_ref, wbim_ref, abre_ref, abim_ref, wcre_ref, wcim_ref,
               d_ref, gw_ref, gb_ref, o_ref, xre_ref, xim_ref, hre_ref, him_ref):
    @pl.when(pl.program_id(0) == 0)
    def _():
        hre_ref[...] = jnp.zeros_like(hre_ref)
        him_ref[...] = jnp.zeros_like(him_ref)

    rows = u_ref.shape[0]
    n_steps = rows // SUBLANES
    u = u_ref[...]
    u_bf = u.astype(BF16)
    hs = S5_HALF_STATES
    y_halves = []
    for half in range(2):
        uh = u_bf[:, half * S5_HALF_CH:(half + 1) * S5_HALF_CH]
        xre_ref[...] = _dot(uh, wbre_ref[half])
        xim_ref[...] = _dot(uh, wbim_ref[half])
        a_re = jnp.broadcast_to(abre_ref[:, half * hs:(half + 1) * hs], (SUBLANES, hs))
        a_im = jnp.broadcast_to(abim_ref[:, half * hs:(half + 1) * hs], (SUBLANES, hs))

        def step(t, carry):
            h_re, h_im = carry
            r0 = pl.multiple_of(t * SUBLANES, SUBLANES)
            n_re = a_re * h_re - a_im * h_im + xre_ref[pl.ds(r0, SUBLANES), :]
            n_im = a_re * h_im + a_im * h_re + xim_ref[pl.ds(r0, SUBLANES), :]
            xre_ref[pl.ds(r0, SUBLANES), :] = n_re
            xim_ref[pl.ds(r0, SUBLANES), :] = n_im
            return n_re, n_im

        h0 = (hre_ref[:, half * hs:(half + 1) * hs], him_ref[:, half * hs:(half + 1) * hs])
        h_re, h_im = lax.fori_loop(0, n_steps, step, h0, unroll=4)
        hre_ref[:, half * hs:(half + 1) * hs] = h_re
        him_ref[:, half * hs:(half + 1) * hs] = h_im
        y_halves.append(_dot(xre_ref[...].astype(BF16), wcre_ref[half])
                        - _dot(xim_ref[...].astype(BF16), wcim_ref[half]))
    y = jnp.concatenate(y_halves, axis=-1) + d_ref[...] * u
    y = _gelu_tanh(y)
    gate = _sigmoid(_dot(y.astype(BF16), gw_ref[...]) + gb_ref[...])
    o_ref[...] = (y * gate).astype(o_ref.dtype)


def _s5_mixer(u_tb, batch, seq, wb_re, wb_im, ab_re, ab_im, wc_re, wc_im, d_skip, glu_w, glu_b):
    rows = S5_TIME_BLOCK * batch
    width = S5_WIDTH
    return pl.pallas_call(
        _s5_kernel,
        out_shape=jax.ShapeDtypeStruct((seq * batch, width), BF16),
        grid=(seq // S5_TIME_BLOCK,),
        in_specs=[pl.BlockSpec((rows, width), lambda i: (i, 0))] + [_vmem_spec()] * 9,
        out_specs=pl.BlockSpec((rows, width), lambda i: (i, 0)),
        scratch_shapes=[pltpu.VMEM((rows, S5_HALF_STATES), F32),
                        pltpu.VMEM((rows, S5_HALF_STATES), F32),
                        pltpu.VMEM((batch, 2 * S5_HALF_STATES), F32),
                        pltpu.VMEM((batch, 2 * S5_HALF_STATES), F32)],
        compiler_params=_params("arbitrary"),
        name="s5_mixer",
    )(u_tb, wb_re, wb_im, ab_re, ab_im, wc_re, wc_im, d_skip, glu_w, glu_b)


def _ssd_kernel(zs_ref, xbc_ref, tail_ref, dtraw_ref, convw_ref, convb_ref, dtb_ref, alog_ref, dexp_ref,
                normw_ref, tril_ref, expand_ref, o_ref, state_ref):
    q = SSD_CHUNK
    p = SSD_HEADDIM
    first = pl.program_id(1) == 0

    @pl.when(first)
    def _():
        state_ref[...] = jnp.zeros_like(state_ref)

    x = xbc_ref[...]
    tail = tail_ref[...] * jnp.where(first, 0.0, 1.0)
    row8 = lax.broadcasted_iota(jnp.int32, (SUBLANES, SSD_CONV_CH), 0)
    conv = convb_ref[...] + convw_ref[SSD_CONV - 1:SSD_CONV, :] * x
    for k in range(SSD_CONV - 1):
        back = SSD_CONV - 1 - k
        rolled = pltpu.roll(x, back, 0)
        top = jnp.where(row8 < back, pltpu.roll(tail, back, 0), rolled[0:SUBLANES])
        shifted = jnp.concatenate([top, rolled[SUBLANES:]], axis=0)
        conv = conv + convw_ref[k:k + 1, :] * shifted
    act = _silu(conv)
    xs = act[:, :SSD_WIDTH]
    b_bf = act[:, SSD_WIDTH:SSD_WIDTH + SSD_BC_WIDTH].astype(BF16)
    c_bf = act[:, SSD_WIDTH + SSD_BC_WIDTH:].astype(BF16)

    dt = _softplus(dtraw_ref[...] + dtb_ref[...])
    dt_a = dt * (-jnp.exp(alog_ref[...]) * LOG2E)
    hp = dt.shape[1]
    cs3 = _dot(tril_ref[...], jnp.concatenate(_split3(dt_a), axis=1))
    cs = cs3[:, :hp] + cs3[:, hp:2 * hp] + cs3[:, 2 * hp:]
    cs_t = cs.T
    cs_e = _dot(jnp.concatenate(_split3(cs), axis=1), expand_ref[...])
    dt_hi = dt.astype(BF16)
    dt_lo = (dt - dt_hi.astype(F32)).astype(BF16)
    dt_e = _dot(jnp.concatenate([dt_hi, dt_lo], axis=1), expand_ref[0:2 * hp, :])
    xdt = xs * dt_e
    xdt_bf = xdt.astype(BF16)
    cs_last_e = cs_e[q - 1:q, :]
    decay_in = jnp.exp2(cs_e)
    chunk_decay = jnp.exp2(cs_last_e)
    x_end = (xdt * jnp.exp2(cs_last_e - cs_e)).astype(BF16)

    row_i = lax.broadcasted_iota(jnp.int32, (q, q), 0)
    col_i = lax.broadcasted_iota(jnp.int32, (q, q), 1)
    causal = row_i >= col_i
    first_of_pair = lax.broadcasted_iota(jnp.int32, (q, 2 * p), 1) < p

    y_groups = []
    for g in range(SSD_GROUPS):
        bg = b_bf[:, g * SSD_STATE:(g + 1) * SSD_STATE]
        cg = c_bf[:, g * SSD_STATE:(g + 1) * SSD_STATE]
        gsl = slice(g * SSD_GROUP_WIDTH, (g + 1) * SSD_GROUP_WIDTH)
        cb = _dot_nt(cg, bg)
        state = state_ref[g]
        y_off = _dot(cg, state.astype(BF16)) * decay_in[:, gsl]
        y_pairs = []
        for pr in range(SSD_HPG // 2):
            h0 = g * SSD_HPG + 2 * pr
            xp = xdt_bf[:, h0 * p:(h0 + 2) * p]
            parts = []
            for h in (h0, h0 + 1):
                seg = jnp.broadcast_to(cs[:, h:h + 1], (q, q)) - jnp.broadcast_to(cs_t[h:h + 1, :], (q, q))
                decay = jnp.exp2(jnp.where(causal, seg, NEG_BIG))
                parts.append(_dot((cb * decay).astype(BF16), xp))
            y_pairs.append(jnp.where(first_of_pair, parts[0], parts[1]))
        y_groups.append(jnp.concatenate(y_pairs, axis=-1) + y_off)
        new_state = _dot_tn(bg, x_end[:, gsl])
        state_ref[g] = state * chunk_decay[:, gsl] + new_state

    y = jnp.concatenate(y_groups, axis=-1) + dexp_ref[...] * xs
    y = y * zs_ref[...]
    outs = []
    for g in range(SSD_GROUPS):
        gsl = slice(g * SSD_GROUP_WIDTH, (g + 1) * SSD_GROUP_WIDTH)
        yg = y[:, gsl]
        ms = jnp.mean(yg * yg, axis=-1, keepdims=True)
        outs.append(yg * lax.rsqrt(ms + RMS_EPS) * normw_ref[:, gsl])
    o_ref[...] = jnp.concatenate(outs, axis=-1).astype(o_ref.dtype)


def _ssd_mixer(zs, xbc, dt_raw, batch, seq, conv_w, conv_b, dt_bias, a_log, d_skip, norm_w):
    q = SSD_CHUNK
    n_c = seq // q
    heads_pad = LANES
    pad_h = heads_pad - SSD_HEADS
    dtb = jnp.pad(dt_bias, (0, pad_h)).reshape(1, heads_pad)
    alog = jnp.pad(a_log, (0, pad_h)).reshape(1, heads_pad)
    d_exp = jnp.repeat(d_skip, SSD_HEADDIM).reshape(1, SSD_WIDTH)
    t = jnp.arange(q)
    tril = (t[:, None] >= t[None, :]).astype(BF16)
    expand = (jnp.arange(heads_pad)[:, None] == (jnp.arange(SSD_WIDTH) // SSD_HEADDIM)[None, :]).astype(BF16)
    expand3 = jnp.tile(expand, (3, 1))
    row_map = lambda b, c: (b * n_c + c, 0)
    tiles_per_chunk = q // SUBLANES
    tail_map = lambda b, c: (jnp.maximum((b * n_c + c) * tiles_per_chunk - 1, 0), 0)
    return pl.pallas_call(
        _ssd_kernel,
        out_shape=jax.ShapeDtypeStruct((batch * seq, SSD_WIDTH), BF16),
        grid=(batch, n_c),
        in_specs=[pl.BlockSpec((q, SSD_WIDTH), row_map),
                  pl.BlockSpec((q, SSD_CONV_CH), row_map),
                  pl.BlockSpec((SUBLANES, SSD_CONV_CH), tail_map),
                  pl.BlockSpec((q, heads_pad), row_map)] + [_vmem_spec()] * 8,
        out_specs=pl.BlockSpec((q, SSD_WIDTH), row_map),
        scratch_shapes=[pltpu.VMEM((SSD_GROUPS, SSD_STATE, SSD_GROUP_WIDTH), F32)],
        compiler_params=_params("parallel", "arbitrary"),
        name="ssd_mixer",
    )(zs, xbc, xbc, dt_raw, conv_w, conv_b.reshape(1, SSD_CONV_CH), dtb, alog, d_exp,
      norm_w.reshape(1, SSD_WIDTH), tril, expand3)


def _head_rmsnorm(x, w_tiled, seg_ones, head_dim):
    sq = x * x
    hi = sq.astype(BF16)
    lo = (sq - hi.astype(F32)).astype(BF16)
    width = x.shape[1]
    blk = seg_ones.shape[0]
    parts = []
    for c0 in range(0, width, blk):
        parts.append(_dot(hi[:, c0:c0 + blk], seg_ones) + _dot(lo[:, c0:c0 + blk], seg_ones))
    ss = parts[0] if len(parts) == 1 else jnp.concatenate(parts, axis=-1)
    return x * lax.rsqrt(ss * (1.0 / head_dim) + RMS_EPS) * w_tiled


def _swa_kernel(q_ref, k_ref, v_ref, kprev_ref, vprev_ref, qw_ref, kw_ref, sink_ref, seg_ref, rep_ref, o_ref):
    blk = SWA_WINDOW
    hd = SWA_HEADDIM
    slab = SWA_HPG * hd
    first = pl.program_id(1) == 0

    n_blk = q_ref.shape[0] // blk
    seg_ones = seg_ref[...]
    qn = _head_rmsnorm(q_ref[...], qw_ref[...], seg_ones, hd) * (hd ** -0.5)
    k_both = jnp.concatenate([kprev_ref[...], k_ref[...]], axis=0)
    k_cat = _head_rmsnorm(k_both, kw_ref[...], seg_ones, hd).astype(BF16)
    v_cat = jnp.concatenate([vprev_ref[...], v_ref[...]], axis=0).astype(BF16)

    qpos = lax.broadcasted_iota(jnp.int32, (blk, blk), 0)
    kpos = lax.broadcasted_iota(jnp.int32, (blk, blk), 1)
    in_prev = kpos > qpos
    prev_bias = jnp.where(first, NEG_BIG, 0.0)
    lane = lax.broadcasted_iota(jnp.int32, (blk, slab), 1)
    head_masks = [(lane >= r * hd) & (lane < (r + 1) * hd) for r in range(SWA_HPG)]

    scores, v_reps = {}, []
    for g in range(SWA_KV_HEADS):
        rep = rep_ref[g]
        k_rep = _dot(k_cat, rep).astype(BF16)
        v_reps.append(_dot(v_cat, rep).astype(BF16))
        for j in range(n_blk):
            q_slab = qn[j * blk:(j + 1) * blk, g * slab:(g + 1) * slab]
            q_stack = jnp.concatenate([jnp.where(mk, q_slab, 0.0) for mk in head_masks], axis=0).astype(BF16)
            scores[g, j] = _dot_nt(q_stack, k_rep[j * blk:(j + 2) * blk])

    probs = {}
    for g in range(SWA_KV_HEADS):
        for j in range(n_blk):
            p_heads = []
            for r in range(SWA_HPG):
                s = scores[g, j][r * blk:(r + 1) * blk, :]
                s_prev = s[:, :blk] + prev_bias if j == 0 else s[:, :blk]
                s_fold = jnp.where(in_prev, s_prev, s[:, blk:])
                sink = sink_ref[g * SWA_HPG + r]
                m = jnp.maximum(jnp.max(s_fold, axis=-1, keepdims=True), sink)
                p = jnp.exp(s_fold - m)
                denom = jnp.sum(p, axis=-1, keepdims=True) + jnp.exp(sink - m)
                pn = p * (1.0 / denom)
                p_heads.append(jnp.concatenate([jnp.where(in_prev, pn, 0.0).astype(BF16),
                                                jnp.where(in_prev, 0.0, pn).astype(BF16)], axis=1))
            probs[g, j] = jnp.concatenate(p_heads, axis=0)

    for g in range(SWA_KV_HEADS):
        for j in range(n_blk):
            o_stack = _dot(probs[g, j], v_reps[g][j * blk:(j + 2) * blk])
            o_slab = jnp.where(head_masks[0], o_stack[0:blk], 0.0)
            for r in range(1, SWA_HPG):
                o_slab = o_slab + jnp.where(head_masks[r], o_stack[r * blk:(r + 1) * blk], 0.0)
            o_ref[j * blk:(j + 1) * blk, g * slab:(g + 1) * slab] = o_slab.astype(o_ref.dtype)


def _swa_mixer(q, k, v, batch, seq, q_norm_w, k_norm_w, sinks):
    blk = SWA_WINDOW
    n_blk = SWA_BLOCKS_PER_STEP
    tq = n_blk * blk
    n_b = seq // tq
    qw = jnp.tile(q_norm_w, SWA_HEADS).reshape(1, SWA_WIDTH)
    kw = jnp.tile(k_norm_w, SWA_KV_HEADS).reshape(1, SWA_KV_WIDTH)
    seg = SWA_KV_WIDTH
    lane = jnp.arange(seg)
    seg_ones = (lane[:, None] // SWA_HEADDIM == lane[None, :] // SWA_HEADDIM).astype(BF16)
    rep = (lane[None, :, None] == (jnp.arange(SWA_KV_HEADS)[:, None, None] * SWA_HEADDIM
                                   + lane[None, None, :] % SWA_HEADDIM)).astype(BF16)
    row_map = lambda b, i: (b * n_b + i, 0)
    prev_map = lambda b, i: (jnp.maximum((b * n_b + i) * n_blk - 1, 0), 0)
    return pl.pallas_call(
        _swa_kernel,
        out_shape=jax.ShapeDtypeStruct((batch * seq, SWA_WIDTH), BF16),
        grid=(batch, n_b),
        in_specs=[pl.BlockSpec((tq, SWA_WIDTH), row_map),
                  pl.BlockSpec((tq, SWA_KV_WIDTH), row_map),
                  pl.BlockSpec((tq, SWA_KV_WIDTH), row_map),
                  pl.BlockSpec((blk, SWA_KV_WIDTH), prev_map),
                  pl.BlockSpec((blk, SWA_KV_WIDTH), prev_map),
                  _vmem_spec(), _vmem_spec(),
                  pl.BlockSpec(memory_space=pltpu.SMEM),
                  _vmem_spec(), _vmem_spec()],
        out_specs=pl.BlockSpec((tq, SWA_WIDTH), row_map),
        compiler_params=_params("parallel", "parallel"),
        name="swa_mixer",
    )(q, k, v, k, v, qw, kw, sinks, seg_ones, rep)


def _gla_kernel(q_ref, k_ref, v_ref, gate_ref, glr_ref, w2_ref, gb_ref, normw_ref, tril_ref,
                o_ref, state_ref):
    c = GLA_CHUNK
    dk, dv = GLA_HEAD_K, GLA_HEAD_V
    rows = q_ref.shape[0]

    @pl.when(pl.program_id(1) == 0)
    def _():
        state_ref[...] = jnp.zeros_like(state_ref)

    w2 = w2_ref[...]
    g_hi, g_mid, g_lo = _split3(glr_ref[...])
    w_hi, w_mid, w_lo = _split3(w2)
    pre = (_dot(g_hi, w_hi) + _dot(g_hi, w_mid) + _dot(g_mid, w_hi)
           + _dot(g_hi, w_lo) + _dot(g_mid, w_mid) + _dot(g_lo, w_hi)) + gb_ref[...]
    gk = -_softplus(-pre) * (LOG2E / GLA_GATE_NORMALIZER)
    gk3 = _dot(tril_ref[...], jnp.concatenate(_split3(gk), axis=1))
    kw = gk.shape[1]
    bc = gk3[:, :kw] + gk3[:, kw:2 * kw] + gk3[:, 2 * kw:]

    n_chunks = rows // c
    pair = 2 * c
    assert rows % pair == 0
    b_lasts = [bc[(i + 1) * c - 1:(i + 1) * c, :] for i in range(n_chunks)]
    b_last = jnp.concatenate([jnp.broadcast_to(b, (c, kw)) for b in b_lasts], axis=0)
    k_all = k_ref[...]
    q_t = (q_ref[...] * (dk ** -0.5) * jnp.exp2(bc)).astype(BF16)
    k_t = (k_all * jnp.exp2(-bc)).astype(BF16)
    k_end = (k_all * jnp.exp2(b_last - bc)).astype(BF16)
    decs = [jnp.exp2(b) for b in b_lasts]
    v_bf = v_ref[...].astype(BF16)
    row_i = lax.broadcasted_iota(jnp.int32, (pair, pair), 0)
    col_i = lax.broadcasted_iota(jnp.int32, (pair, pair), 1)
    same_chunk_causal = (row_i >= col_i) & (col_i >= (row_i // c) * c)

    heads = []
    for h in range(GLA_HEADS):
        ksl = slice(h * dk, (h + 1) * dk)
        vsl = slice(h * dv, (h + 1) * dv)
        o_intra = []
        for r0 in range(0, rows, pair):
            rsl = slice(r0, r0 + pair)
            att = jnp.where(same_chunk_causal, _dot_nt(q_t[rsl, ksl], k_t[rsl, ksl]), 0.0).astype(BF16)
            o_intra.append(_dot(att, v_bf[rsl, vsl]))
        state = state_ref[h]
        o_inter = []
        for i in range(n_chunks):
            rsl = slice(i * c, (i + 1) * c)
            o_inter.append(_dot_nt(q_t[rsl, ksl], state.astype(BF16)))
            state = state * decs[i][:, ksl] + _dot_tn(v_bf[rsl, vsl], k_end[rsl, ksl])
        state_ref[h] = state
        o = jnp.concatenate(o_intra, axis=0) + jnp.concatenate(o_inter, axis=0)
        ms = jnp.mean(o * o, axis=-1, keepdims=True)
        heads.append(o * lax.rsqrt(ms + RMS_EPS) * normw_ref[...])
    o_ref[...] = (jnp.concatenate(heads, axis=-1) * gate_ref[...]).astype(o_ref.dtype)


def _gla_mixer(q, k, v, gate, g_lr, batch, seq, gate_w2, gate_b, norm_w):
    tq = GLA_TIME_BLOCK
    n_t = seq // tq
    rank_pad = LANES
    w2 = jnp.pad(gate_w2, ((0, rank_pad - GLA_GATE_RANK), (0, 0)))
    idx = jnp.arange(tq)
    tril = ((idx[:, None] >= idx[None, :]) & (idx[:, None] // GLA_CHUNK == idx[None, :] // GLA_CHUNK)).astype(BF16)
    row_map = lambda b, i: (b * n_t + i, 0)
    return pl.pallas_call(
        _gla_kernel,
        out_shape=jax.ShapeDtypeStruct((batch * seq, GLA_VALUE_WIDTH), BF16),
        grid=(batch, n_t),
        in_specs=[pl.BlockSpec((tq, GLA_KEY_WIDTH), row_map),
                  pl.BlockSpec((tq, GLA_KEY_WIDTH), row_map),
                  pl.BlockSpec((tq, GLA_VALUE_WIDTH), row_map),
                  pl.BlockSpec((tq, GLA_VALUE_WIDTH), row_map),
                  pl.BlockSpec((tq, rank_pad), row_map)] + [_vmem_spec()] * 4,
        out_specs=pl.BlockSpec((tq, GLA_VALUE_WIDTH), row_map),
        scratch_shapes=[pltpu.VMEM((GLA_HEADS, GLA_HEAD_V, GLA_HEAD_K), F32)],
        compiler_params=_params("parallel", "arbitrary"),
        name="gla_mixer",
    )(q, k, v, gate, g_lr, w2, gate_b.reshape(1, GLA_KEY_WIDTH), norm_w.reshape(1, GLA_HEAD_V), tril)


def _pad_cols(w, total):
    return jnp.pad(w, ((0, 0), (0, total - w.shape[1])))


def _layer0(h2d, batch, seq, norm_mix, w_in, s5_A_re, s5_A_im, s5_log_dt, s5_B_re, s5_B_im, s5_C_re,
            s5_C_im, s5_D, s5_glu_w, s5_glu_b, ssd_conv_w, ssd_conv_b, ssd_dt_bias, ssd_A_log,
            ssd_D, ssd_norm_w, w_out, norm_ffn, w1, w3, w2):
    z0 = S5_WIDTH
    x0 = z0 + SSD_WIDTH
    dt0 = x0 + SSD_CONV_CH
    pieces = [(0, S5_WIDTH, PLAIN), (z0, SSD_WIDTH, SILU), (x0, SSD_CONV_CH, PLAIN), (dt0, LANES, PLAIN)]
    w_in_p = _pad_cols(w_in.astype(BF16), dt0 + LANES)
    u, zs, xbc, dt_raw = _norm_proj(h2d, norm_mix, w_in_p, pieces, 0, batch, seq)

    ab_re, ab_im, bc_re, bc_im = _s5_discretize(s5_A_re, s5_A_im, s5_log_dt, s5_B_re, s5_B_im)
    wb_re = _block_diag_halves(bc_re).astype(BF16)
    wb_im = _block_diag_halves(bc_im).astype(BF16)
    wc_re = _block_diag_halves(jnp.transpose(s5_C_re, (0, 2, 1))).astype(BF16)
    wc_im = _block_diag_halves(jnp.transpose(s5_C_im, (0, 2, 1))).astype(BF16)
    ya = _s5_mixer(u.reshape(seq * batch, S5_WIDTH), batch, seq, wb_re, wb_im,
                   ab_re.reshape(1, -1), ab_im.reshape(1, -1), wc_re, wc_im,
                   s5_D.reshape(1, S5_WIDTH), s5_glu_w.astype(BF16), s5_glu_b.reshape(1, S5_WIDTH))
    ya = ya.reshape(seq, batch * S5_WIDTH)

    yb = _ssd_mixer(zs, xbc, dt_raw, batch, seq, ssd_conv_w, ssd_conv_b, ssd_dt_bias, ssd_A_log,
                    ssd_D, ssd_norm_w)
    h2d = _out_proj([ya, yb], [True, False], h2d, w_out.astype(BF16), batch, seq)
    return _ffn(h2d, norm_ffn, w1.astype(BF16), w3.astype(BF16), w2.astype(BF16))


def _layer1(h2d, batch, seq, norm_mix, w_in, swa_q_norm, swa_k_norm, swa_sinks, gla_gate_w2,
            gla_gate_b, gla_norm_w, w_out, norm_ffn, w1, w3, w2):
    widths = (SWA_WIDTH, SWA_KV_WIDTH, SWA_KV_WIDTH, GLA_KEY_WIDTH, GLA_KEY_WIDTH,
              GLA_VALUE_WIDTH, GLA_VALUE_WIDTH, LANES)
    starts = np.concatenate([[0], np.cumsum(widths)[:-1]]).tolist()
    kinds = (PLAIN,) * 6 + (SILU, PLAIN)
    pieces = list(zip(starts, widths, kinds))
    w_in_p = _pad_cols(w_in.astype(BF16), starts[-1] + LANES)
    qc, kc, vc, qd, kd, vd, gate, g_lr = _norm_proj(h2d, norm_mix, w_in_p, pieces, None, batch, seq)
    yc = _swa_mixer(qc, kc, vc, batch, seq, swa_q_norm, swa_k_norm, swa_sinks)
    yd = _gla_mixer(qd, kd, vd, gate, g_lr, batch, seq, gla_gate_w2, gla_gate_b, gla_norm_w)
    h2d = _out_proj([yc, yd], [False, False], h2d, w_out.astype(BF16), batch, seq)
    return _ffn(h2d, norm_ffn, w1.astype(BF16), w3.astype(BF16), w2.astype(BF16))


def kernel(x, norm0_mix, w_in0, s5_A_re, s5_A_im, s5_log_dt, s5_B_re, s5_B_im, s5_C_re, s5_C_im, s5_D, s5_glu_w, s5_glu_b, ssd_conv_w, ssd_conv_b, ssd_dt_bias, ssd_A_log, ssd_D, ssd_norm_w, w_out0, norm0_ffn, ffn0_w1, ffn0_w3, ffn0_w2, norm1_mix, w_in1, swa_q_norm, swa_k_norm, swa_sinks, gla_gate_w2, gla_gate_b, gla_norm_w, w_out1, norm1_ffn, ffn1_w1, ffn1_w3, ffn1_w2):
    batch, seq, d = x.shape
    assert batch == SUBLANES and d == D_MODEL and seq % OUT_ROW_BLOCK == 0
    h2d = x.reshape(batch * seq, d)
    h2d = _layer0(h2d, batch, seq, norm0_mix, w_in0, s5_A_re, s5_A_im, s5_log_dt, s5_B_re, s5_B_im,
                  s5_C_re, s5_C_im, s5_D, s5_glu_w, s5_glu_b, ssd_conv_w, ssd_conv_b, ssd_dt_bias,
                  ssd_A_log, ssd_D, ssd_norm_w, w_out0, norm0_ffn, ffn0_w1, ffn0_w3, ffn0_w2)
    h2d = _layer1(h2d, batch, seq, norm1_mix, w_in1, swa_q_norm, swa_k_norm, swa_sinks, gla_gate_w2,
                  gla_gate_b, gla_norm_w, w_out1, norm1_ffn, ffn1_w1, ffn1_w3, ffn1_w2)
    return h2d.reshape(batch, seq, d)
```
